```python
import math
import jax, jax.numpy as jnp
from jax import lax
import numpy as np

D_MODEL = 1024
BATCH = 4
SEQ = 8192
DEPTH = 2
DEC_BATCH = 32
DEC_SEQ = 1
PAST_LEN = 16384
PAGE_SIZE = 128

HD = 64
H_A = 8
KV_A = 4
R_A = H_A // KV_A
H_B = 16
KV_B = 2
R_B = H_B // KV_B
CMP_LEN = 32
CMP_STRIDE = 16
CMP_HID = 128
SEL_BLOCK = 64
SEL_TOPN = 16
WINDOW = 512
N_BUCKETS = 32
MAX_DIST = 128
D_FF = 2816
N_EXPERTS = 8
TOP_K = 2
D_FF_E = 2816
Q_BLOCK = 128
EPS = 1e-6
NEG = -1e30
FORCE = 1e9

A_OUT = H_A * 2 * HD
B_OUT = H_B * HD
SPLITS = (H_A * 2 * HD, KV_A * 2 * HD, KV_A * 2 * HD, H_B * HD, 6 * KV_B * HD, 3 * H_B, 2 * D_MODEL)
N_IN = sum(SPLITS)

kernel_name = 'gated_diffattn_nsa_moe_decoder_step'


def rmsnorm(x, g):
    xf = x.astype(jnp.float32)
    y = xf * lax.rsqrt(jnp.mean(xf * xf, axis=-1, keepdims=True) + EPS)
    return (y * g.astype(jnp.float32)).astype(x.dtype)


def rel_bucket(dist):
    d = jnp.maximum(dist, 0)
    exact = N_BUCKETS // 2
    df = jnp.maximum(d, 1).astype(jnp.float32)
    far = exact + (jnp.log(df / exact) / math.log(MAX_DIST / exact) * (N_BUCKETS - exact)).astype(jnp.int32)
    return jnp.where(d < exact, d, jnp.minimum(far, N_BUCKETS - 1))


def head_bias(table, dist, n_groups):
    b = jnp.moveaxis(table[rel_bucket(dist)].astype(jnp.float32), -1, 0)
    return b.reshape(n_groups, -1, *dist.shape)


def masked_softmax(s, mask):
    s = jnp.where(mask, s, NEG)
    e = jnp.where(mask, jnp.exp(s - jnp.max(s, axis=-1, keepdims=True)), 0.0)
    return e / jnp.maximum(jnp.sum(e, axis=-1, keepdims=True), 1e-30)


def project(h, w):
    B, T, _ = h.shape
    offs = np.cumsum(SPLITS)[:-1].tolist()
    qa, ka, va, qb, kvb, gb, gm = jnp.split(h @ w, offs, axis=-1)
    return (qa.reshape(B, T, H_A, 2, HD), ka.reshape(B, T, KV_A, 2 * HD), va.reshape(B, T, KV_A, 2 * HD),
            qb.reshape(B, T, H_B, HD), kvb.reshape(B, T, 6, KV_B, HD),
            jax.nn.sigmoid(gb).reshape(B, T, H_B, 3), jax.nn.sigmoid(gm).reshape(B, T, 2, D_MODEL))


def diff_lambda_value(lp, lam_init):
    lp = lp.astype(jnp.float32)
    return jnp.exp(jnp.sum(lp[0] * lp[1])) - jnp.exp(jnp.sum(lp[2] * lp[3])) + lam_init


def diff_attn(q, k, v, q_pos, k_pos, bias_tab, lam):
    B, Tq = q.shape[:2]
    qg = q.reshape(B, Tq, KV_A, R_A, 2, HD)
    s = jnp.einsum('bqgrcd,bkgcd->bgrcqk', qg, k).astype(jnp.float32) * (HD ** -0.5)
    dist = q_pos[:, None] - k_pos[None, :]
    s = s + head_bias(bias_tab, dist, KV_A)[:, :, None]
    p = masked_softmax(s, dist >= 0)
    a = p[:, :, :, 0] - lam * p[:, :, :, 1]
    o = jnp.einsum('bgrqk,bkgd->bqgrd', a.astype(v.dtype), v)
    return o.reshape(B, Tq, H_A, 2 * HD)


def diff_post(o, subln, lam_init):
    B, T = o.shape[:2]
    return (rmsnorm(o, subln) * (1.0 - lam_init)).reshape(B, T, A_OUT)


def compress(x, pe, w1, w2):
    B, L, G, _ = x.shape
    n_sub = CMP_LEN // CMP_STRIDE
    pieces = x.reshape(B, L // CMP_STRIDE, CMP_STRIDE, G, HD)
    nc = L // CMP_STRIDE - n_sub + 1
    blocks = jnp.concatenate([pieces[:, j:j + nc] for j in range(n_sub)], axis=2)
    blocks = blocks + pe[:, None, :]
    flat = jnp.swapaxes(blocks, 2, 3).reshape(B, nc, G, CMP_LEN * HD)
    return jax.nn.gelu(flat @ w1) @ w2


def cmp_ends(nc):
    return jnp.arange(nc) * CMP_STRIDE + (CMP_LEN - 1)


def cmp_to_sel(nc, ns):
    start = jnp.arange(nc)[:, None] * CMP_STRIDE
    lo = jnp.arange(ns)[None, :] * SEL_BLOCK
    return ((start < lo + SEL_BLOCK) & (start + CMP_LEN > lo)).astype(jnp.float32)


def to_blocks(x):
    B, L, G, _ = x.shape
    return x.reshape(B, L // SEL_BLOCK, SEL_BLOCK, G, HD).transpose(0, 3, 1, 2, 4)


def nsa_attend(q, gates, q_pos, kc, vc, c_end, ks, vs, kw, vw, w_pos, bias_tab, c2s):
    B, Tq = q.shape[:2]
    ns = ks.shape[2]
    scale = HD ** -0.5
    qg = q.reshape(B, Tq, KV_B, R_B, HD)
    dist_c = q_pos[:, None] - c_end[None, :]
    s_c = jnp.einsum('bqgrd,bngd->bgrqn', qg, kc).astype(jnp.float32) * scale + head_bias(bias_tab, dist_c, KV_B)
    p_c = masked_softmax(s_c, dist_c >= 0)
    o_c = jnp.einsum('bgrqn,bngd->bqgrd', p_c.astype(vc.dtype), vc)
    imp = jnp.einsum('bgrqn,ns->bgqs', p_c, c2s)
    blk = jnp.arange(ns)[None, :]
    cur = (q_pos // SEL_BLOCK)[:, None]
    forced = (blk == 0) | (blk == cur) | (blk == cur - 1)
    score = jnp.where(blk <= cur, jnp.where(forced, FORCE, imp), NEG)
    _, idx = lax.top_k(score, min(SEL_TOPN, ns))
    bi = jnp.arange(B)[:, None, None, None]
    gi = jnp.arange(KV_B)[None, :, None, None]
    n_keys = idx.shape[-1] * SEL_BLOCK
    k_sel = ks[bi, gi, idx].reshape(B, KV_B, Tq, n_keys, HD)
    v_sel = vs[bi, gi, idx].reshape(B, KV_B, Tq, n_keys, HD)
    pos = (idx[..., None] * SEL_BLOCK + jnp.arange(SEL_BLOCK)).reshape(B, KV_B, Tq, n_keys)
    dist_s = q_pos[:, None] - pos
    bias_s = bias_tab.reshape(N_BUCKETS, KV_B, R_B)[rel_bucket(dist_s), gi]
    bias_s = jnp.moveaxis(bias_s.astype(jnp.float32), -1, 2)
    s_s = jnp.einsum('bqgrd,bgqmd->bgrqm', qg, k_sel).astype(jnp.float32) * scale + bias_s
    p_s = masked_softmax(s_s, (dist_s >= 0)[:, :, None])
    o_s = jnp.einsum('bgrqm,bgqmd->bqgrd', p_s.astype(v_sel.dtype), v_sel)
    dist_w = q_pos[:, None] - w_pos[None, :]
    mask_w = (dist_w >= 0) & (dist_w <= WINDOW) & (w_pos[None, :] >= 0)
    s_w = jnp.einsum('bqgrd,bkgd->bgrqk', qg, kw).astype(jnp.float32) * scale + head_bias(bias_tab, dist_w, KV_B)
    p_w = masked_softmax(s_w, mask_w)
    o_w = jnp.einsum('bgrqk,bkgd->bqgrd', p_w.astype(vw.dtype), vw)
    g = gates.reshape(B, Tq, KV_B, R_B, 3)
    o = g[..., 0:1] * o_c + g[..., 1:2] * o_s + g[..., 2:3] * o_w
    return o.reshape(B, Tq, B_OUT)


def merge(a, b, gm, w_a, w_b, w_o):
    return (gm[..., 0, :] * (a @ w_a) + gm[..., 1, :] * (b @ w_b)) @ w_o


def mixer_prompt(h, w_in, w_a, w_b, w_o, lam, lam_init, subln, bias_a, bias_b, cmp_pe, cmp_w1, cmp_w2):
    B, T, _ = h.shape
    qa, ka, va, qb, kvb, gb, gm = project(h, w_in)
    n_blk = T // Q_BLOCK
    pos = jnp.arange(T)
    ka4 = ka.reshape(B, T, KV_A, 2, HD)

    def attn_block(i):
        q = lax.dynamic_slice_in_dim(qa, i * Q_BLOCK, Q_BLOCK, axis=1)
        return diff_attn(q, ka4, va, i * Q_BLOCK + jnp.arange(Q_BLOCK), pos, bias_a, lam)

    oa = jnp.swapaxes(lax.map(attn_block, jnp.arange(n_blk)), 0, 1).reshape(B, T, H_A, 2 * HD)
    a = diff_post(oa, subln, lam_init)

    kc = compress(kvb[:, :, 0], cmp_pe[0], cmp_w1[0], cmp_w2[0])
    vc = compress(kvb[:, :, 1], cmp_pe[1], cmp_w1[1], cmp_w2[1])
    nc, ns = kc.shape[1], T // SEL_BLOCK
    c_end, c2s = cmp_ends(nc), cmp_to_sel(nc, ns)
    ks, vs = to_blocks(kvb[:, :, 2]), to_blocks(kvb[:, :, 3])
    pad = ((0, 0), (WINDOW, 0), (0, 0), (0, 0))
    kw, vw = jnp.pad(kvb[:, :, 4], pad), jnp.pad(kvb[:, :, 5], pad)

    def nsa_block(i):
        s0 = i * Q_BLOCK
        q = lax.dynamic_slice_in_dim(qb, s0, Q_BLOCK, axis=1)
        g = lax.dynamic_slice_in_dim(gb, s0, Q_BLOCK, axis=1)
        kwb = lax.dynamic_slice_in_dim(kw, s0, WINDOW + Q_BLOCK, axis=1)
        vwb = lax.dynamic_slice_in_dim(vw, s0, WINDOW + Q_BLOCK, axis=1)
        w_pos = s0 - WINDOW + jnp.arange(WINDOW + Q_BLOCK)
        return nsa_attend(q, g, s0 + jnp.arange(Q_BLOCK), kc, vc, c_end, ks, vs, kwb, vwb, w_pos, bias_b, c2s)

    ob = jnp.swapaxes(lax.map(nsa_block, jnp.arange(n_blk)), 0, 1).reshape(B, T, B_OUT)
    y = merge(a, ob, gm, w_a, w_b, w_o)
    return y, jnp.stack([ka, va], axis=2), kvb[:, :, :4], kvb[:, T - min(WINDOW, T):, 4:]


def mixer_sample(h, attn_past, nsa_past, win_past, w_in, w_a, w_b, w_o, lam, lam_init, subln, bias_a, bias_b,
                 cmp_pe, cmp_w1, cmp_w2):
    B, T, _ = h.shape
    qa, ka, va, qb, kvb, gb, gm = project(h, w_in)
    past = attn_past.shape[1]
    L = past + T
    q_pos = past + jnp.arange(T)
    k_all = jnp.concatenate([attn_past[:, :, 0], ka], axis=1).reshape(B, L, KV_A, 2, HD)
    v_all = jnp.concatenate([attn_past[:, :, 1], va], axis=1)
    a = diff_post(diff_attn(qa, k_all, v_all, q_pos, jnp.arange(L), bias_a, lam), subln, lam_init)

    lp = -(-L // SEL_BLOCK) * SEL_BLOCK
    rows = jnp.concatenate([nsa_past, kvb[:, :, :4]], axis=1)
    rows = jnp.pad(rows, ((0, 0), (0, lp - L), (0, 0), (0, 0), (0, 0)))
    kc = compress(rows[:, :, 0], cmp_pe[0], cmp_w1[0], cmp_w2[0])
    vc = compress(rows[:, :, 1], cmp_pe[1], cmp_w1[1], cmp_w2[1])
    nc, ns = kc.shape[1], lp // SEL_BLOCK
    win = jnp.concatenate([win_past, kvb[:, :, 4:]], axis=1)
    n_win = win_past.shape[1]
    w_pos = past - n_win + jnp.arange(n_win + T)
    ob = nsa_attend(qb, gb, q_pos, kc, vc, cmp_ends(nc), to_blocks(rows[:, :, 2]), to_blocks(rows[:, :, 3]),
                    win[:, :, 0], win[:, :, 1], w_pos, bias_b, cmp_to_sel(nc, ns))
    y = merge(a, ob, gm, w_a, w_b, w_o)
    return y, jnp.stack([ka, va], axis=2), kvb[:, :, :4], win[:, T:]


def gather_past(cache, l, page_table):
    rows = cache[l, page_table]
    return rows.reshape(rows.shape[0], rows.shape[1] * rows.shape[2], *rows.shape[3:])


def swiglu(h, w1, w3, w2):
    return (jax.nn.silu(h @ w1) * (h @ w3)) @ w2


def moe_ffn(h, w_r, w1, w3, w2):
    logits = (h @ w_r).astype(jnp.float32)
    top_v, top_i = lax.top_k(logits, TOP_K)
    wts = jax.nn.softmax(top_v, axis=-1)
    gate = jnp.sum(jax.nn.one_hot(top_i, N_EXPERTS, dtype=jnp.float32) * wts[..., None], axis=-2).astype(h.dtype)
    y = jnp.zeros_like(h)
    for e in range(N_EXPERTS):
        y = y + gate[..., e:e + 1] * swiglu(h, w1[e], w3[e], w2[e])
    return y


def setup_inputs(seed: int = 0) -> dict:
    key = jax.random.key(seed)
    ks = jax.random.split(key, 32)
    n_pages = PAST_LEN // PAGE_SIZE
    used = DEC_BATCH * n_pages
    n_pool = used + max(1, used // 4)
    win_buf = min(WINDOW, PAST_LEN)
    n_dense = (DEPTH + 1) // 2
    n_moe = DEPTH // 2

    def nrm(k, shape, scale=1.0):
        return jax.random.normal(k, shape, jnp.float32) * scale

    page_table = jax.random.permutation(ks[0], n_pool)[:used].reshape(DEC_BATCH, n_pages).astype(jnp.int32)
    return {
        'x_prompt': nrm(ks[1], (BATCH, SEQ, D_MODEL)),
        'x_sample': nrm(ks[2], (DEC_BATCH, DEC_SEQ, D_MODEL)),
        'cache_attn_kv': nrm(ks[3], (DEPTH, n_pool, PAGE_SIZE, 2, KV_A, 2 * HD)),
        'cache_nsa_kv': nrm(ks[4], (DEPTH, n_pool, PAGE_SIZE, 4, KV_B, HD)),
        'state_win_kv': nrm(ks[5], (DEPTH, DEC_BATCH, win_buf, 2, KV_B, HD)),
        'page_table': page_table,
        'rel_bias': nrm(ks[6], (N_BUCKETS, H_A + H_B), 0.2),
        'norm_mix': 1.0 + nrm(ks[7], (DEPTH, D_MODEL), 0.02),
        'w_in': nrm(ks[8], (DEPTH, D_MODEL, N_IN), D_MODEL ** -0.5),
        'w_a': nrm(ks[9], (DEPTH, A_OUT, D_MODEL), A_OUT ** -0.5),
        'w_b': nrm(ks[10], (DEPTH, B_OUT, D_MODEL), B_OUT ** -0.5),
        'w_o': nrm(ks[11], (DEPTH, D_MODEL, D_MODEL), D_MODEL ** -0.5),
        'diff_lambda': nrm(ks[12], (DEPTH, 4, HD), 0.1),
        'diff_subln': 1.0 + nrm(ks[13], (DEPTH, 2 * HD), 0.02),
        'cmp_pe': nrm(ks[14], (DEPTH, 2, CMP_LEN, HD), 0.1),
        'cmp_w1': nrm(ks[15], (DEPTH, 2, CMP_LEN * HD, CMP_HID), (CMP_LEN * HD) ** -0.5),
        'cmp_w2': nrm(ks[16], (DEPTH, 2, CMP_HID, HD), CMP_HID ** -0.5),
        'norm_ffn': 1.0 + nrm(ks[17], (DEPTH, D_MODEL), 0.02),
        'ffn_w1': nrm(ks[18], (n_dense, D_MODEL, D_FF), D_MODEL ** -0.5),
        'ffn_w3': nrm(ks[19], (n_dense, D_MODEL, D_FF), D_MODEL ** -0.5),
        'ffn_w2': nrm(ks[20], (n_dense, D_FF, D_MODEL), D_FF ** -0.5),
        'moe_router': nrm(ks[21], (n_moe, D_MODEL, N_EXPERTS), D_MODEL ** -0.5),
        'moe_w1': nrm(ks[22], (n_moe, N_EXPERTS, D_MODEL, D_FF_E), D_MODEL ** -0.5),
        'moe_w3': nrm(ks[23], (n_moe, N_EXPERTS, D_MODEL, D_FF_E), D_MODEL ** -0.5),
        'moe_w2': nrm(ks[24], (n_moe, N_EXPERTS, D_FF_E, D_MODEL), D_FF_E ** -0.5),
        'norm_final': 1.0 + nrm(ks[25], (D_MODEL,), 0.02),
    }


def reference(x_prompt, x_sample, cache_attn_kv, cache_nsa_kv, state_win_kv, page_table, rel_bias, norm_mix,
              w_in, w_a, w_b, w_o, diff_lambda, diff_subln, cmp_pe, cmp_w1, cmp_w2, norm_ffn, ffn_w1, ffn_w3,
              ffn_w2, moe_router, moe_w1, moe_w3, moe_w2, norm_final):
    bias_a = rel_bias[:, :H_A]
    bias_b = rel_bias[:, H_A:]
    xp, xs = x_prompt, x_sample
    attn_p, attn_s, nsa_p, nsa_s, win_p, win_s = [], [], [], [], [], []
    for l in range(DEPTH):
        lam_init = 0.8 - 0.6 * math.exp(-0.3 * l)
        lam = diff_lambda_value(diff_lambda[l], lam_init)
        wts = (w_in[l], w_a[l], w_b[l], w_o[l], lam, lam_init, diff_subln[l], bias_a, bias_b,
               cmp_pe[l], cmp_w1[l], cmp_w2[l])
        yp, ap, npr, wp = mixer_prompt(rmsnorm(xp, norm_mix[l]), *wts)
        ys, a_s, nss, wss = mixer_sample(rmsnorm(xs, norm_mix[l]), gather_past(cache_attn_kv, l, page_table),
                                         gather_past(cache_nsa_kv, l, page_table), state_win_kv[l], *wts)
        xp = xp + yp
        xs = xs + ys
        attn_p.append(ap); attn_s.append(a_s); nsa_p.append(npr); nsa_s.append(nss)
        win_p.append(wp); win_s.append(wss)
        e = l // 2
        if l % 2 == 0:
            xp = xp + swiglu(rmsnorm(xp, norm_ffn[l]), ffn_w1[e], ffn_w3[e], ffn_w2[e])
            xs = xs + swiglu(rmsnorm(xs, norm_ffn[l]), ffn_w1[e], ffn_w3[e], ffn_w2[e])
        else:
            xp = xp + moe_ffn(rmsnorm(xp, norm_ffn[l]), moe_router[e], moe_w1[e], moe_w3[e], moe_w2[e])
            xs = xs + moe_ffn(rmsnorm(xs, norm_ffn[l]), moe_router[e], moe_w1[e], moe_w3[e], moe_w2[e])
    y_prompt = rmsnorm(xp, norm_final)
    y_sample = rmsnorm(xs, norm_final)
    attn_kv_prompt = jnp.stack(attn_p)
    attn_kv_sample = jnp.stack(attn_s)
    nsa_kv_prompt = jnp.stack(nsa_p)
    nsa_kv_sample = jnp.stack(nsa_s)
    win_kv_prompt = jnp.stack(win_p)
    win_kv_sample = jnp.stack(win_s)
    return (y_prompt, y_sample, attn_kv_prompt, attn_kv_sample, nsa_kv_prompt, nsa_kv_sample, win_kv_prompt, win_kv_sample)
```

```python
import functools
import math

import numpy as np
import jax
import jax.numpy as jnp
from jax import lax
from jax.experimental import pallas as pl
from jax.experimental.pallas import tpu as pltpu

F32 = jnp.float32
BF16 = jnp.bfloat16

HD = 64
H_A = 8
KV_A = 4
H_B = 16
KV_B = 2
R_B = H_B // KV_B
CMP_LEN = 32
CMP_STRIDE = 16
CMP_HID = 128
SEL_BLOCK = 64
SEL_TOPN = 16
WINDOW = 512
N_BUCKETS = 32
MAX_DIST = 128
N_EXPERTS = 8
PAGE = 128
EPS = 1e-6
NEG = -1e30
FORCE = 1e9
REMOVED = -3e38

LANES = 128
VMEM_LIMIT = 56 * 1024 * 1024

ATT_T = 512
NSA_TQ = 128
NSA_TK = 512
WIN_W = WINDOW + NSA_TQ
PAGES_PER_STEP = 8

N_ROW = 3968
N_T = 1792


def _cparams(sem):
    return pltpu.CompilerParams(dimension_semantics=sem, vmem_limit_bytes=VMEM_LIMIT)


def _resident(block, index_map):
    return pl.BlockSpec(block, index_map, pipeline_mode=pl.Buffered(1))


def _rms(x, g):
    xf = x.astype(F32)
    return xf * lax.rsqrt(jnp.mean(xf * xf, axis=-1, keepdims=True) + EPS) * g


def _dot(a, b):
    return jnp.dot(a, b, preferred_element_type=F32)


def _dot_nt(a, b):
    return lax.dot_general(a, b, (((1,), (1,)), ((), ())), preferred_element_type=F32)


def _bucket_np(d):
    d = np.maximum(np.asarray(d, np.int64), 0)
    exact = N_BUCKETS // 2
    df = np.maximum(d, 1).astype(np.float32)
    far = exact + (np.log(df / np.float32(exact)) / np.float32(math.log(MAX_DIST / exact))
                   * np.float32(N_BUCKETS - exact)).astype(np.int64)
    return np.where(d < exact, d, np.minimum(far, N_BUCKETS - 1))


FAR_DIST = 128
assert np.all(_bucket_np(np.arange(FAR_DIST, 1 << 16)) == N_BUCKETS - 1)


def _bias_of_dist(table, dist):
    idx = jnp.asarray(_bucket_np(dist).astype(np.int32))
    return jnp.moveaxis(jnp.take(table, idx, axis=0), -1, 0)


def _proj_kernel(x_ref, g_ref, wr_ref, wt_ref, qa_ref, kva_ref, va_ref, qb_ref, cmp_ref, vdup_ref, gb_ref,
                 kat_ref, nsat_ref, wint_ref, kdupt_ref):
    tm = x_ref.shape[0]
    hb = _rms(x_ref[...], g_ref[...]).astype(BF16)

    def cols(a, b):
        return _dot(hb, wr_ref[:, a:b])

    z = cols(0, 1024)
    for h in range(H_A):
        qa_ref[h] = z[:, h * LANES:(h + 1) * LANES].astype(BF16)
    z = cols(1024, 2048)
    for c in range(2 * KV_A):
        kva_ref[pl.ds(c, tm, stride=2 * KV_A), :] = z[:, c * LANES:(c + 1) * LANES]
    va_ref[...] = z[:, 512:1024].astype(BF16)
    z = cols(2048, 3072)
    for p in range(H_B // 2):
        qb_ref[p] = z[:, p * LANES:(p + 1) * LANES].astype(BF16)
    cmp_ref[...] = cols(3072, 3328)
    z = cols(3328, 3840)
    for k in range(4):
        vdup_ref[k] = z[:, k * LANES:(k + 1) * LANES].astype(BF16)
    gb_ref[...] = jax.nn.sigmoid(cols(3840, N_ROW))

    def rows(a, b):
        return _dot_nt(wt_ref[a:b, :], hb)

    z = rows(0, 512)
    for g in range(KV_A):
        kat_ref[g] = z[g * LANES:(g + 1) * LANES].astype(BF16)
    nsat_ref[...] = rows(512, 1024)
    wint_ref[...] = rows(1024, 1280)
    z = rows(1280, N_T)
    for k in range(4):
        kdupt_ref[k] = z[k * LANES:(k + 1) * LANES].astype(BF16)


def _proj(x, gain, w_row, w_t, tm):
    Bp, T, D = x.shape
    nt = T // tm
    n = Bp * T
    row = lambda b, i: (b * nt + i, 0)
    out_shape = (
        jax.ShapeDtypeStruct((Bp, H_A, T, LANES), BF16),
        jax.ShapeDtypeStruct((n * 2 * KV_A, LANES), F32),
        jax.ShapeDtypeStruct((n, KV_A * 2 * HD), BF16),
        jax.ShapeDtypeStruct((Bp, H_B // 2, T, LANES), BF16),
        jax.ShapeDtypeStruct((n, 4 * HD), F32),
        jax.ShapeDtypeStruct((Bp, 4, T, LANES), BF16),
        jax.ShapeDtypeStruct((n, LANES), F32),
        jax.ShapeDtypeStruct((Bp, KV_A, LANES, T), BF16),
        jax.ShapeDtypeStruct((Bp, 512, T), F32),
        jax.ShapeDtypeStruct((Bp, 256, T), F32),
        jax.ShapeDtypeStruct((Bp, 4, LANES, T), BF16),
    )
    out_specs = (
        pl.BlockSpec((None, H_A, tm, LANES), lambda b, i: (b, 0, i, 0)),
        pl.BlockSpec((tm * 2 * KV_A, LANES), row),
        pl.BlockSpec((tm, 512), row),
        pl.BlockSpec((None, H_B // 2, tm, LANES), lambda b, i: (b, 0, i, 0)),
        pl.BlockSpec((tm, 256), row),
        pl.BlockSpec((None, 4, tm, LANES), lambda b, i: (b, 0, i, 0)),
        pl.BlockSpec((tm, LANES), row),
        pl.BlockSpec((None, KV_A, LANES, tm), lambda b, i: (b, 0, 0, i)),
        pl.BlockSpec((None, 512, tm), lambda b, i: (b, 0, i)),
        pl.BlockSpec((None, 256, tm), lambda b, i: (b, 0, i)),
        pl.BlockSpec((None, 4, LANES, tm), lambda b, i: (b, 0, 0, i)),
    )
    return pl.pallas_call(
        _proj_kernel,
        grid=(Bp, nt),
        in_specs=[
            pl.BlockSpec((None, tm, D), lambda b, i: (b, i, 0)),
            pl.BlockSpec((1, D), lambda b, i: (0, 0)),
            _resident((D, N_ROW), lambda b, i: (0, 0)),
            _resident((N_T, D), lambda b, i: (0, 0)),
        ],
        out_specs=out_specs,
        out_shape=out_shape,
        compiler_params=_cparams(("parallel", "parallel")),
        name="proj",
    )(x, gain.reshape(1, D), w_row, w_t)


def _prep_in_weights(w_in_l):
    wt = w_in_l.T
    D = wt.shape[1]
    scale = HD ** -0.5
    qa, ka, va = wt[0:1024] * scale, wt[1024:1536], wt[1536:2048]
    qb = wt[2048:3072] * scale
    kvb = wt[3072:3840].reshape(6, KV_B, HD, D)
    gb = wt[3840:3888]
    gm = wt[3888:]
    dup = lambda w: jnp.concatenate([w, w], axis=0)
    vdup = jnp.concatenate([dup(kvb[k, g]) for k in (3, 5) for g in range(KV_B)], axis=0)
    kdup = jnp.concatenate([dup(kvb[k, g]) for k in (2, 4) for g in range(KV_B)], axis=0)
    w_row_t = jnp.concatenate(
        [qa, ka, va, qb, kvb[0:2].reshape(256, D), vdup, gb, jnp.zeros((LANES - gb.shape[0], D), F32)], axis=0)
    w_t = jnp.concatenate([ka, kvb[0:4].reshape(512, D), kvb[4:6].reshape(256, D), kdup], axis=0)
    return w_row_t.T.astype(BF16), w_t.astype(BF16), gm.T.astype(BF16)


def _softmax_step(s, v, m_ref, l_ref, acc_ref, c):
    m_old = m_ref[c]
    m_new = jnp.maximum(m_old, jnp.max(s, axis=-1, keepdims=True))
    alpha = jnp.exp(m_old - m_new)
    p = jnp.exp(s - m_new)
    l_ref[c] = alpha * l_ref[c] + jnp.sum(p, axis=-1, keepdims=True)
    acc_ref[c] = alpha * acc_ref[c] + _dot(p.astype(BF16), v)
    m_ref[c] = m_new


def _diff_lambda(lp, lam_init):
    a = jnp.sum(lp[0:1] * lp[1:2], axis=-1, keepdims=True)
    b = jnp.sum(lp[2:3] * lp[3:4], axis=-1, keepdims=True)
    return jnp.exp(a) - jnp.exp(b) + lam_init


def _diff_finish(o0, o1, lp, subln, lam_init):
    o = o0 - _diff_lambda(lp, lam_init) * o1
    return _rms(o, subln) * (1.0 - lam_init)


def _dattn_kernel(far_ref, q_ref, kt_ref, v_ref, bias_ref, lam_ref, sub_ref, o_ref, m_ref, l_ref, acc_ref, *,
                  tq, lam_init):
    g = pl.program_id(1)
    qi = pl.program_id(2)
    q = q_ref[...].reshape(2 * tq, LANES)
    lane = lax.broadcasted_iota(jnp.int32, (2 * tq, LANES), 1)
    zero = jnp.zeros_like(q)
    qv = (jnp.where(lane < HD, q, zero), jnp.where(lane >= HD, q, zero))
    row = lax.broadcasted_iota(jnp.int32, (2 * tq, 1), 0)
    far_col = jnp.where(row < tq, far_ref[2 * g], far_ref[2 * g + 1])
    m_ref[...] = jnp.full(m_ref.shape, NEG, F32)
    l_ref[...] = jnp.zeros(l_ref.shape, F32)
    acc_ref[...] = jnp.zeros(acc_ref.shape, F32)

    def step(j, bias):
        start = pl.multiple_of(j * tq, tq)
        kt = kt_ref[:, pl.ds(start, tq)]
        v = v_ref[pl.ds(start, tq), :]
        for c in range(2):
            _softmax_step(_dot(qv[c], kt) + bias, v, m_ref, l_ref, acc_ref, c)

    def far_body(j, carry):
        step(j, far_col)
        return carry

    lax.fori_loop(0, jnp.maximum(qi - 1, 0), far_body, 0)

    @pl.when(qi >= 1)
    def _():
        step(qi - 1, bias_ref[:, 1].reshape(2 * tq, tq))

    step(qi, bias_ref[:, 0].reshape(2 * tq, tq))

    o0 = acc_ref[0] / jnp.maximum(l_ref[0], 1e-30)
    o1 = acc_ref[1] / jnp.maximum(l_ref[1], 1e-30)
    y = _diff_finish(o0, o1, lam_ref[...], sub_ref[...], lam_init).astype(BF16)
    o_ref[:, 0:LANES] = y[:tq]
    o_ref[:, LANES:2 * LANES] = y[tq:]


def _dattn_bias_tiles(bias_a, t):
    i = np.arange(t)[:, None]
    j = np.arange(t)[None, :]
    d0 = jnp.where(jnp.asarray(i >= j), _bias_of_dist(bias_a, i - j), NEG)
    d1 = _bias_of_dist(bias_a, t + i - j)
    return jnp.stack([d0, d1], axis=1)


def _dattn(qa4, kat, va, bias_a, lam_p, subln, lam_init, tq):
    B, _, T, _ = qa4.shape
    nq = T // tq
    tiles = _dattn_bias_tiles(bias_a, tq)
    far = bias_a[N_BUCKETS - 1]
    return pl.pallas_call(
        functools.partial(_dattn_kernel, tq=tq, lam_init=lam_init),
        grid=(B, KV_A, nq),
        in_specs=[
            pl.BlockSpec(memory_space=pltpu.SMEM),
            pl.BlockSpec((None, 2, tq, LANES), lambda b, g, i: (b, g, i, 0)),
            pl.BlockSpec((None, None, LANES, T), lambda b, g, i: (b, g, 0, 0)),
            pl.BlockSpec((T, LANES), lambda b, g, i: (b, g)),
            pl.BlockSpec((2, 2, tq, tq), lambda b, g, i: (g, 0, 0, 0)),
            pl.BlockSpec((4, HD), lambda b, g, i: (0, 0)),
            pl.BlockSpec((1, 2 * HD), lambda b, g, i: (0, 0)),
        ],
        out_specs=pl.BlockSpec((tq, 2 * LANES), lambda b, g, i: (b * nq + i, g)),
        out_shape=jax.ShapeDtypeStruct((B * T, H_A * 2 * HD), BF16),
        scratch_shapes=[pltpu.VMEM((2, 2 * tq, 1), F32), pltpu.VMEM((2, 2 * tq, 1), F32),
                        pltpu.VMEM((2, 2 * tq, LANES), F32)],
        compiler_params=_cparams(("parallel", "parallel", "arbitrary")),
        name="dattn",
    )(far, qa4, kat, va, tiles, lam_p, subln.reshape(1, 2 * HD))


CMP_K = CMP_LEN * KV_B * HD


def _compress_core(xs_ref, nblk, pe_ref, w1_ref, w2_ref, p_ref):
    for j in range(CMP_LEN):
        xj = xs_ref[pl.ds(j, nblk, stride=CMP_STRIDE), :]
        p_ref[:, j * LANES:(j + 1) * LANES] = (xj + pe_ref[:, j * LANES:(j + 1) * LANES]).astype(BF16)
    hid = jax.nn.gelu(_dot(p_ref[...], w1_ref[...]))
    return _dot(hid.astype(BF16), w2_ref[...])


def _compress_kernel(x_ref, pe_ref, w1_ref, w2_ref, o_ref, xs_ref, p_ref, *, T):
    xs_ref[0:T, :] = x_ref[...]
    xs_ref[T:T + CMP_STRIDE, :] = jnp.zeros((CMP_STRIDE, LANES), F32)
    res = _compress_core(xs_ref, T // CMP_STRIDE, pe_ref, w1_ref, w2_ref, p_ref)
    o_ref[0] = res[:, 0:LANES].astype(BF16)
    o_ref[1] = res[:, LANES:2 * LANES].astype(BF16)


def _prep_cmp_weights(pe, w1, w2):
    pe_flat = jnp.broadcast_to(pe[:, :, None, :], (2, CMP_LEN, KV_B, HD)).reshape(2, 1, CMP_K)
    w1r = w1.reshape(2, CMP_LEN, HD, CMP_HID)
    w1bd = jnp.zeros((2, CMP_LEN, KV_B, HD, KV_B, CMP_HID), F32)
    w2bd = jnp.zeros((2, KV_B, CMP_HID, KV_B, 2, HD), F32)
    for g in range(KV_B):
        w1bd = w1bd.at[:, :, g, :, g, :].set(w1r)
        w2bd = w2bd.at[:, g, :, g, :, :].set(jnp.broadcast_to(w2[:, :, None, :], (2, CMP_HID, 2, HD)))
    return (pe_flat, w1bd.reshape(2, CMP_K, KV_B * CMP_HID).astype(BF16),
            w2bd.reshape(2, KV_B * CMP_HID, KV_B * 2 * HD).astype(BF16))


def _compress(cmp_rows, B, T, pe_flat, w1bd, w2bd):
    nblk = T // CMP_STRIDE
    return pl.pallas_call(
        functools.partial(_compress_kernel, T=T),
        grid=(B, 2),
        in_specs=[
            pl.BlockSpec((T, LANES), lambda b, k: (b, k)),
            pl.BlockSpec((None, 1, CMP_K), lambda b, k: (k, 0, 0)),
            pl.BlockSpec((None, CMP_K, 2 * CMP_HID), lambda b, k: (k, 0, 0)),
            pl.BlockSpec((None, 2 * CMP_HID, 2 * LANES), lambda b, k: (k, 0, 0)),
        ],
        out_specs=pl.BlockSpec((None, None, KV_B, nblk, LANES), lambda b, k: (b, k, 0, 0, 0)),
        out_shape=jax.ShapeDtypeStruct((B, 2, KV_B, nblk, LANES), BF16),
        scratch_shapes=[pltpu.VMEM((T + CMP_STRIDE, LANES), F32), pltpu.VMEM((nblk, CMP_K), BF16)],
        compiler_params=_cparams(("parallel", "parallel")),
        name="compress",
    )(cmp_rows, pe_flat, w1bd, w2bd)


def _group_head(g, rb):
    return R_B * g + 2 * (rb % 4) + rb // 4


def _far_column(far_ref, g, tq):
    rb = lax.broadcasted_iota(jnp.int32, (R_B * tq, 1), 0) // tq
    col = jnp.zeros((R_B * tq, 1), F32)
    for k in range(R_B):
        col = jnp.where(rb == k, far_ref[_group_head(g, k)], col)
    return col


def _stack_query(q4):
    lane = lax.broadcasted_iota(jnp.int32, q4.shape, 1)
    zero = jnp.zeros_like(q4)
    return jnp.concatenate([jnp.where(lane < HD, q4, zero), jnp.where(lane >= HD, q4, zero)], axis=0)


def _top_blocks(score, n):
    lanef = lax.broadcasted_iota(jnp.int32, score.shape, 1).astype(F32)
    width = float(score.shape[1])

    def body(_, carry):
        sc, sel = carry
        mx = jnp.max(sc, axis=1, keepdims=True)
        idx = jnp.min(jnp.where(sc == mx, lanef, width), axis=1, keepdims=True)
        hit = lanef == idx
        return jnp.where(hit, REMOVED, sc), jnp.where(hit, 1.0, sel)

    return lax.fori_loop(0, n, body, (score, jnp.zeros_like(score)))[1]


def _unstack(acc, l, tq):
    o = acc / jnp.maximum(l, 1e-30)
    lane = lax.broadcasted_iota(jnp.int32, (4 * tq, LANES), 1)
    return jnp.where(lane < HD, o[:4 * tq], o[4 * tq:])


def _nsa_kernel(far_ref, q_ref, gb_ref, kcv_ref, kselt_ref, kwint_ref, vsel_ref, vwin_ref, d_ref, ac_ref,
                c2s_ref, ex_ref, o_ref, s_ref, m_ref, l_ref, acc_ref, *, tq, tk, T):
    qi = pl.program_id(1)
    p0 = qi * tq
    nc = kcv_ref.shape[2]
    rows = R_B * tq
    ql = lax.broadcasted_iota(jnp.int32, (rows, 1), 0) % tq
    qpos = p0 + ql

    gates = gb_ref[...]
    g_hi = gates.astype(BF16)
    g_lo = (gates - g_hi.astype(F32)).astype(BF16)
    gexp = [_dot(g_hi, ex_ref[br]) + _dot(g_lo, ex_ref[br]) for br in range(3)]

    def add_tile(kind, off):
        for rb in range(R_B):
            s_ref[rb * tq:(rb + 1) * tq, pl.ds(off, tq)] += d_ref[_group_head(g, rb), kind]

    for g in range(KV_B):
        Q = _stack_query(q_ref[4 * g:4 * g + 4].reshape(4 * tq, LANES))
        far_col = _far_column(far_ref, g, tq)

        st = (tq // CMP_STRIDE) * qi - LANES // 2
        wi = lax.broadcasted_iota(jnp.int32, (LANES, nc), 0)
        ci = lax.broadcasted_iota(jnp.int32, (LANES, nc), 1)
        place = jnp.where(ci == wi + st, 1.0, 0.0).astype(BF16)
        s = _dot_nt(Q, kcv_ref[0, g]) + far_col + _dot(ac_ref[0, g], place) + _dot(ac_ref[1, g], place)
        cend = lax.broadcasted_iota(jnp.int32, (1, nc), 1) * CMP_STRIDE + (CMP_LEN - 1)
        mask = cend <= qpos
        s = jnp.where(mask, s, NEG)
        e = jnp.where(mask, jnp.exp(s - jnp.max(s, axis=-1, keepdims=True)), 0.0)
        pc = (e / jnp.maximum(jnp.sum(e, axis=-1, keepdims=True), 1e-30)).astype(BF16)
        oc = _dot(pc, kcv_ref[1, g])
        lane4 = lax.broadcasted_iota(jnp.int32, (4 * tq, LANES), 1)
        o_c = jnp.where(lane4 < HD, oc[:4 * tq], oc[4 * tq:])
        imp = jnp.sum(_dot(pc, c2s_ref[...]).reshape(R_B, tq, LANES), axis=0)

        blk = lax.broadcasted_iota(jnp.int32, (tq, LANES), 1)
        cur = (p0 + lax.broadcasted_iota(jnp.int32, (tq, 1), 0)) // SEL_BLOCK
        forced = (blk == 0) | (blk == cur) | (blk == cur - 1)
        score = jnp.where(blk <= cur, jnp.where(forced, FORCE, imp), NEG)
        sel = _top_blocks(score, SEL_TOPN)
        selneg = jnp.where((sel > 0.5) & (blk <= cur), 0.0, NEG).astype(BF16)

        m_ref[...] = jnp.full(m_ref.shape, NEG, F32)
        l_ref[...] = jnp.zeros(l_ref.shape, F32)
        acc_ref[...] = jnp.zeros(acc_ref.shape, F32)
        bi = lax.broadcasted_iota(jnp.int32, (LANES, tk), 0)
        kb = lax.broadcasted_iota(jnp.int32, (LANES, tk), 1) // SEL_BLOCK

        def sel_block(j, near):
            ks = pl.multiple_of(j * tk, tk)
            s = _dot(Q, kselt_ref[g, :, pl.ds(ks, tk)]) + far_col
            expand = jnp.where(bi - kb == j * (tk // SEL_BLOCK), 1.0, 0.0).astype(BF16)
            madd = _dot(selneg, expand)
            s = (s.reshape(R_B, tq, tk) + madd[None]).reshape(rows, tk)
            if near:
                s_ref[:, 0:tk] = s
                off0 = p0 - ks

                @pl.when((off0 >= 0) & (off0 < tk))
                def _():
                    add_tile(0, pl.multiple_of(off0, tq))

                @pl.when((off0 >= tq) & (off0 < tk + tq))
                def _():
                    add_tile(1, pl.multiple_of(off0 - tq, tq))

                key = ks + lax.broadcasted_iota(jnp.int32, (1, tk), 1)
                s = jnp.where(key <= qpos, s_ref[:, 0:tk], NEG)
            _softmax_step(s, vsel_ref[g, pl.ds(ks, tk), :], m_ref, l_ref, acc_ref, 0)

        jd = p0 // tk

        def far_body(j, carry):
            sel_block(j, False)
            return carry

        lax.fori_loop(0, jnp.maximum(jd - 1, 0), far_body, 0)

        @pl.when(jd >= 1)
        def _():
            sel_block(jd - 1, True)

        sel_block(jd, True)
        o_s = _unstack(acc_ref[0], l_ref[0], tq)

        ws = pl.multiple_of(jnp.maximum(p0 - WINDOW, 0), tq)
        ww = WINDOW + tq
        s_ref[:, 0:ww] = _dot(Q, kwint_ref[g, :, pl.ds(ws, ww)]) + far_col
        off0 = p0 - ws
        add_tile(0, pl.multiple_of(off0, tq))

        @pl.when(off0 >= tq)
        def _():
            add_tile(1, pl.multiple_of(off0 - tq, tq))

        dist = qpos - (ws + lax.broadcasted_iota(jnp.int32, (1, ww), 1))
        mask = (dist >= 0) & (dist <= WINDOW)
        s = jnp.where(mask, s_ref[:, 0:ww], NEG)
        e = jnp.where(mask, jnp.exp(s - jnp.max(s, axis=-1, keepdims=True)), 0.0)
        ow = _dot(e.astype(BF16), vwin_ref[g, pl.ds(ws, ww), :])
        o_w = _unstack(ow, jnp.sum(e, axis=-1, keepdims=True), tq)

        for p in range(4):
            c0 = (4 * g + p) * LANES
            r = slice(p * tq, (p + 1) * tq)
            o_ref[:, c0:c0 + LANES] = (gexp[0][:, c0:c0 + LANES] * o_c[r] + gexp[1][:, c0:c0 + LANES] * o_s[r]
                                       + gexp[2][:, c0:c0 + LANES] * o_w[r]).astype(BF16)


def _nsa_tables(bias_b, tq, nc, ns_pad):
    far = bias_b[N_BUCKETS - 1]
    i = np.arange(tq)[:, None]
    j = np.arange(tq)[None, :]
    d0 = jnp.where(jnp.asarray(i >= j), _bias_of_dist(bias_b, i - j) - far[:, None, None], 0.0)
    d1 = _bias_of_dist(bias_b, tq + i - j) - far[:, None, None]
    tiles = jnp.stack([d0, d1], axis=1)
    w = np.arange(LANES)[None, :]
    dist = i + CMP_STRIDE * (LANES // 2) - CMP_STRIDE * w - (CMP_LEN - 1)
    band = jnp.where(jnp.asarray((dist >= 0) & (dist < FAR_DIST)),
                     _bias_of_dist(bias_b, dist) - far[:, None, None], 0.0)
    order = [[_group_head(g, rb) for rb in range(R_B)] for g in range(KV_B)]
    band = jnp.stack([jnp.concatenate([band[h] for h in order[g]], axis=0) for g in range(KV_B)])
    b_hi = band.astype(BF16)
    b_lo = (band - b_hi.astype(F32)).astype(BF16)
    ac = jnp.stack([b_hi, b_lo])
    ci = np.arange(nc)[:, None] * CMP_STRIDE
    lo = np.arange(ns_pad)[None, :] * SEL_BLOCK
    c2s = jnp.asarray(((ci < lo + SEL_BLOCK) & (ci + CMP_LEN > lo)).astype(np.float32)).astype(BF16)
    k = np.arange(LANES)[:, None]
    col = np.arange(H_B * HD)[None, :]
    ex = jnp.asarray(np.stack([(k == (col // HD) * 3 + br) for br in range(3)]).astype(np.float32)).astype(BF16)
    return far, tiles, ac, c2s, ex


def _nsa(qb4, gb, kcv, kdupt, vdup, bias_b, tq, tk):
    B, _, T, _ = qb4.shape
    nq = T // tq
    nc = kcv.shape[3]
    assert T // SEL_BLOCK <= LANES and T >= WINDOW + tq and T % tk == 0
    far, tiles, ac, c2s, ex = _nsa_tables(bias_b, tq, nc, LANES)
    const = lambda *shape: pl.BlockSpec(shape, lambda b, i: (0,) * len(shape))
    return pl.pallas_call(
        functools.partial(_nsa_kernel, tq=tq, tk=tk, T=T),
        grid=(B, nq),
        in_specs=[
            pl.BlockSpec(memory_space=pltpu.SMEM),
            pl.BlockSpec((None, H_B // 2, tq, LANES), lambda b, i: (b, 0, i, 0)),
            pl.BlockSpec((tq, LANES), lambda b, i: (b * nq + i, 0)),
            _resident((None, 2, KV_B, nc, LANES), lambda b, i: (b, 0, 0, 0, 0)),
            _resident((None, KV_B, LANES, T), lambda b, i: (b, 0, 0, 0)),
            _resident((None, KV_B, LANES, T), lambda b, i: (b, 1, 0, 0)),
            _resident((None, KV_B, T, LANES), lambda b, i: (b, 0, 0, 0)),
            _resident((None, KV_B, T, LANES), lambda b, i: (b, 1, 0, 0)),
            _resident((H_B, 2, tq, tq), lambda b, i: (0, 0, 0, 0)),
            _resident((2, KV_B, R_B * tq, LANES), lambda b, i: (0, 0, 0, 0)),
            _resident((nc, LANES), lambda b, i: (0, 0)),
            _resident((3, LANES, H_B * HD), lambda b, i: (0, 0, 0)),
        ],
        out_specs=pl.BlockSpec((tq, H_B * HD), lambda b, i: (b * nq + i, 0)),
        out_shape=jax.ShapeDtypeStruct((B * T, H_B * HD), BF16),
        scratch_shapes=[pltpu.VMEM((R_B * tq, WINDOW + tq), F32), pltpu.VMEM((1, R_B * tq, 1), F32),
                        pltpu.VMEM((1, R_B * tq, 1), F32), pltpu.VMEM((1, R_B * tq, LANES), F32)],
        compiler_params=_cparams(("parallel", "arbitrary")),
        name="nsa",
    )(far, qb4, gb, kcv, kdupt, kdupt, vdup, vdup, tiles, ac, c2s, ex)


def _merge_kernel(x_ref, g_ref, a_ref, b_ref, wgm_ref, wa_ref, wb_ref, wo_ref, o_ref):
    D = x_ref.shape[1]
    x = x_ref[...]
    hb = _rms(x, g_ref[...]).astype(BF16)
    ya = jax.nn.sigmoid(_dot(hb, wgm_ref[:, 0:D])) * _dot(a_ref[...], wa_ref[...])
    yb = jax.nn.sigmoid(_dot(hb, wgm_ref[:, D:2 * D])) * _dot(b_ref[...], wb_ref[...])
    o_ref[...] = x + _dot((ya + yb).astype(BF16), wo_ref[...])


def _merge(x2, gain, a, ob, w_gm, w_a, w_b, w_o, tm):
    n, D = x2.shape
    row = lambda i: (i, 0)
    return pl.pallas_call(
        _merge_kernel,
        grid=(n // tm,),
        in_specs=[
            pl.BlockSpec((tm, D), row),
            pl.BlockSpec((1, D), lambda i: (0, 0)),
            pl.BlockSpec((tm, a.shape[1]), row),
            pl.BlockSpec((tm, ob.shape[1]), row),
            _resident(w_gm.shape, lambda i: (0, 0)),
            _resident(w_a.shape, lambda i: (0, 0)),
            _resident(w_b.shape, lambda i: (0, 0)),
            _resident(w_o.shape, lambda i: (0, 0)),
        ],
        out_specs=pl.BlockSpec((tm, D), row),
        out_shape=jax.ShapeDtypeStruct((n, D), F32),
        compiler_params=_cparams(("parallel",)),
        name="merge",
    )(x2, gain.reshape(1, D), a, ob, w_gm, w_a, w_b, w_o)


def _swiglu_chunk(hb, w1_ref, w3_ref, w2_ref):
    t = jax.nn.silu(_dot(hb, w1_ref[...])) * _dot(hb, w3_ref[...])
    return _dot(t.astype(BF16), w2_ref[...])


def _ffn_kernel(x_ref, g_ref, w1_ref, w3_ref, w2_ref, gf_ref, o_ref, hb_ref, acc_ref, *, final_norm):
    f = pl.program_id(1)

    @pl.when(f == 0)
    def _():
        hb_ref[...] = _rms(x_ref[...], g_ref[...]).astype(BF16)
        acc_ref[...] = jnp.zeros(acc_ref.shape, F32)

    acc_ref[...] += _swiglu_chunk(hb_ref[...], w1_ref, w3_ref, w2_ref)

    @pl.when(f == pl.num_programs(1) - 1)
    def _():
        y = x_ref[...] + acc_ref[...]
        o_ref[...] = _rms(y, gf_ref[...]) if final_norm else y


def _ffn(x2, gain, w1, w3, w2, gain_final, final_norm, tm, tf):
    n, D = x2.shape
    F = w1.shape[1]
    return pl.pallas_call(
        functools.partial(_ffn_kernel, final_norm=final_norm),
        grid=(n // tm, F // tf),
        in_specs=[
            pl.BlockSpec((tm, D), lambda i, f: (i, 0)),
            pl.BlockSpec((1, D), lambda i, f: (0, 0)),
            pl.BlockSpec((D, tf), lambda i, f: (0, f)),
            pl.BlockSpec((D, tf), lambda i, f: (0, f)),
            pl.BlockSpec((tf, D), lambda i, f: (f, 0)),
            pl.BlockSpec((1, D), lambda i, f: (0, 0)),
        ],
        out_specs=pl.BlockSpec((tm, D), lambda i, f: (i, 0)),
        out_shape=jax.ShapeDtypeStruct((n, D), F32),
        scratch_shapes=[pltpu.VMEM((tm, D), BF16), pltpu.VMEM((tm, D), F32)],
        compiler_params=_cparams(("parallel", "arbitrary")),
        name="ffn",
    )(x2, gain.reshape(1, D), w1, w3, w2, gain_final.reshape(1, D))


def _moe_kernel(x_ref, g_ref, wr_ref, w1_ref, w3_ref, w2_ref, gf_ref, o_ref, hb_ref, gate_ref, acc_ref, *,
                final_norm):
    e = pl.program_id(1)
    f = pl.program_id(2)

    @pl.when((e == 0) & (f == 0))
    def _():
        hb = _rms(x_ref[...], g_ref[...]).astype(BF16)
        hb_ref[...] = hb
        acc_ref[...] = jnp.zeros(acc_ref.shape, F32)
        logits = _dot(hb, wr_ref[...])
        lane = lax.broadcasted_iota(jnp.int32, logits.shape, 1)
        lanef = lane.astype(F32)
        logits = jnp.where(lane < N_EXPERTS, logits, NEG)
        v1 = jnp.max(logits, axis=-1, keepdims=True)
        i1 = jnp.min(jnp.where(logits == v1, lanef, float(LANES)), axis=-1, keepdims=True)
        rest = jnp.where(lanef == i1, REMOVED, logits)
        v2 = jnp.max(rest, axis=-1, keepdims=True)
        i2 = jnp.min(jnp.where(rest == v2, lanef, float(LANES)), axis=-1, keepdims=True)
        e2 = jnp.exp(v2 - v1)
        w_1 = 1.0 / (1.0 + e2)
        w_2 = e2 / (1.0 + e2)
        gate_ref[...] = jnp.where(lanef == i1, w_1, 0.0) + jnp.where(lanef == i2, w_2, 0.0)

    lane = lax.broadcasted_iota(jnp.int32, gate_ref.shape, 1)
    gate_e = jnp.sum(jnp.where(lane == e, gate_ref[...], 0.0), axis=-1, keepdims=True)
    acc_ref[...] += gate_e * _swiglu_chunk(hb_ref[...], w1_ref, w3_ref, w2_ref)

    @pl.when((e == pl.num_programs(1) - 1) & (f == pl.num_programs(2) - 1))
    def _():
        y = x_ref[...] + acc_ref[...]
        o_ref[...] = _rms(y, gf_ref[...]) if final_norm else y


def _moe(x2, gain, w_r, w1, w3, w2, gain_final, final_norm, tm, tf):
    n, D = x2.shape
    E, _, F = w1.shape
    return pl.pallas_call(
        functools.partial(_moe_kernel, final_norm=final_norm),
        grid=(n // tm, E, F // tf),
        in_specs=[
            pl.BlockSpec((tm, D), lambda i, e, f: (i, 0)),
            pl.BlockSpec((1, D), lambda i, e, f: (0, 0)),
            pl.BlockSpec((D, LANES), lambda i, e, f: (0, 0)),
            pl.BlockSpec((None, D, tf), lambda i, e, f: (e, 0, f)),
            pl.BlockSpec((None, D, tf), lambda i, e, f: (e, 0, f)),
            pl.BlockSpec((None, tf, D), lambda i, e, f: (e, f, 0)),
            pl.BlockSpec((1, D), lambda i, e, f: (0, 0)),
        ],
        out_specs=pl.BlockSpec((tm, D), lambda i, e, f: (i, 0)),
        out_shape=jax.ShapeDtypeStruct((n, D), F32),
        scratch_shapes=[pltpu.VMEM((tm, D), BF16), pltpu.VMEM((tm, LANES), F32), pltpu.VMEM((tm, D), F32)],
        compiler_params=_cparams(("parallel", "arbitrary", "arbitrary")),
        name="moe",
    )(x2, gain.reshape(1, D), w_r, w1, w3, w2, gain_final.reshape(1, D))


SROWS = 16


def _page_specs(block, kind_block, layer, n_fixed):
    tail = (0,) * (len(block) - 3)

    def spec(j):
        def index_map(b, c, pt, *_):
            return (layer, pt[b, c * PAGES_PER_STEP + j], kind_block) + tail
        return pl.BlockSpec(block, index_map)

    del n_fixed
    return [spec(j) for j in range(PAGES_PER_STEP)]


def _stack_rows(top, bot):
    pad = jnp.zeros((SROWS - top.shape[0] - bot.shape[0], LANES), F32)
    return jnp.concatenate([top, bot, pad], axis=0)


def _adec_kernel(pt_ref, q_ref, ks_ref, vs_ref, bfar_ref, blast_ref, b0_ref, lam_ref, sub_ref, *rest, lam_init):
    pages = rest[:PAGES_PER_STEP]
    o_ref, qs_ref, m_ref, l_ref, acc_ref = rest[PAGES_PER_STEP:]
    c = pl.program_id(1)
    last = pl.num_programs(1) - 1
    lane2 = lax.broadcasted_iota(jnp.int32, (2, LANES), 1)

    @pl.when(c == 0)
    def _():
        for g in range(KV_A):
            qq = q_ref[2 * g:2 * g + 2]
            qg = _stack_rows(jnp.where(lane2 < HD, qq, 0.0), jnp.where(lane2 >= HD, qq, 0.0))
            qs_ref[g] = qg
            k_self = ks_ref[g:g + 1].astype(BF16).astype(F32)
            s_self = jnp.sum(qg.astype(BF16).astype(F32) * k_self, axis=-1, keepdims=True) + b0_ref[g][:, 0:1]
            m_ref[g] = s_self
            l_ref[g] = jnp.ones((SROWS, 1), F32)
            acc_ref[g] = jnp.broadcast_to(vs_ref[g:g + 1].astype(BF16).astype(F32), (SROWS, LANES))

    for j, page in enumerate(pages):
        for g in range(KV_A):
            kg = page[pl.ds(g, PAGE, stride=2 * KV_A), :].astype(BF16)
            vg = page[pl.ds(KV_A + g, PAGE, stride=2 * KV_A), :].astype(BF16)
            s = _dot_nt(qs_ref[g].astype(BF16), kg)
            if j == PAGES_PER_STEP - 1:
                s = s + jnp.where(c == last, blast_ref[g], bfar_ref[g])
            else:
                s = s + bfar_ref[g]
            _softmax_step(s, vg, m_ref, l_ref, acc_ref, g)

    @pl.when(c == last)
    def _():
        for g in range(KV_A):
            o = acc_ref[g] / jnp.maximum(l_ref[g], 1e-30)
            o_ref[2 * g:2 * g + 2, :] = _diff_finish(o[0:2], o[2:4], lam_ref[...], sub_ref[...], lam_init)


def _head_rows_a(vals):
    out = []
    for g in range(KV_A):
        r = [vals[2 * g], vals[2 * g + 1], vals[2 * g], vals[2 * g + 1]]
        out.append(jnp.stack(r + [jnp.zeros_like(vals[0])] * (SROWS - 4)))
    return jnp.stack(out)


def _head_rows_b(vals):
    out = []
    for g in range(KV_B):
        r = [vals[_group_head(g, rb)] for rb in range(R_B)]
        out.append(jnp.stack(r + [jnp.zeros_like(vals[0])] * (SROWS - R_B)))
    return jnp.stack(out)


def _adec(page_table, cache_view, layer, q, k_self, v_self, bias_a, lam_p, subln, lam_init):
    DB, NP = page_table.shape
    ones = jnp.ones((1, LANES), F32)
    bfar = _head_rows_a(bias_a[N_BUCKETS - 1][:, None] * ones)
    blast = _head_rows_a(_bias_of_dist(bias_a, PAGE - np.arange(PAGE)))
    b0 = _head_rows_a(bias_a[0][:, None] * ones)
    per_b = lambda *blk: pl.BlockSpec((None,) + blk, lambda b, c, pt: (b,) + (0,) * len(blk))
    const = lambda *blk: pl.BlockSpec(blk, lambda b, c, pt: (0,) * len(blk))
    return pl.pallas_call(
        functools.partial(_adec_kernel, lam_init=lam_init),
        grid_spec=pltpu.PrefetchScalarGridSpec(
            num_scalar_prefetch=1,
            grid=(DB, NP // PAGES_PER_STEP),
            in_specs=[per_b(H_A, LANES), per_b(KV_A, LANES), per_b(KV_A, LANES),
                      const(KV_A, SROWS, LANES), const(KV_A, SROWS, LANES), const(KV_A, SROWS, LANES),
                      const(4, HD), const(1, 2 * HD)]
            + _page_specs((None, None, PAGE * 2 * KV_A, LANES), 0, layer, 0),
            out_specs=per_b(H_A, LANES),
            scratch_shapes=[pltpu.VMEM((KV_A, SROWS, LANES), F32), pltpu.VMEM((KV_A, SROWS, 1), F32),
                            pltpu.VMEM((KV_A, SROWS, 1), F32), pltpu.VMEM((KV_A, SROWS, LANES), F32)],
        ),
        out_shape=jax.ShapeDtypeStruct((DB, H_A, LANES), F32),
        compiler_params=_cparams(("parallel", "arbitrary")),
        name="adec",
    )(page_table, q, k_self, v_self, bfar, blast, b0, lam_p, subln.reshape(1, 2 * HD),
      *([cache_view] * PAGES_PER_STEP))


def _stack_pairs(q4):
    lane = lax.broadcasted_iota(jnp.int32, q4.shape, 1)
    return _stack_rows(jnp.where(lane < HD, q4, 0.0), jnp.where(lane >= HD, q4, 0.0))


def _top_indices(score, n):
    lanef = lax.broadcasted_iota(jnp.int32, score.shape, 1).astype(F32)
    slot = lax.broadcasted_iota(jnp.int32, (score.shape[0], LANES), 1)
    width = float(score.shape[1])

    def body(it, carry):
        sc, out = carry
        mx = jnp.max(sc, axis=1, keepdims=True)
        idx = jnp.min(jnp.where(sc == mx, lanef, width), axis=1, keepdims=True)
        return jnp.where(lanef == idx, REMOVED, sc), jnp.where(slot == it, idx.astype(jnp.int32), out)

    return lax.fori_loop(0, n, body, (score, jnp.zeros((score.shape[0], LANES), jnp.int32)))[1]


def _ncmp_kernel(pt_ref, q_ref, new_ref, bc_ref, pe_ref, w1_ref, w2_ref, c2s_ref, *rest, past):
    pages = rest[:PAGES_PER_STEP]
    oc_ref, idx_ref, xk_ref, xv_ref, p_ref = rest[PAGES_PER_STEP:]
    c = pl.program_id(1)
    stage = (xk_ref, xv_ref)
    for j, page in enumerate(pages):
        r0 = pl.multiple_of((c * PAGES_PER_STEP + j) * PAGE, PAGE)
        for kind in range(2):
            stage[kind][pl.ds(r0, PAGE), :] = page[kind].reshape(KV_B * HD, PAGE).T

    @pl.when(c == pl.num_programs(1) - 1)
    def _():
        nblk = past // CMP_STRIDE
        trow = lax.broadcasted_iota(jnp.int32, (2 * CMP_STRIDE, LANES), 0)
        res = []
        for kind in range(2):
            stage[kind][past:past + 2 * CMP_STRIDE, :] = jnp.where(trow == 0, new_ref[kind], 0.0)
            res.append(_compress_core(stage[kind], nblk, pe_ref.at[kind], w1_ref.at[kind], w2_ref.at[kind],
                                      p_ref).astype(BF16))
        ns_pad = c2s_ref.shape[1]
        for g in range(KV_B):
            qg = _stack_pairs(q_ref[4 * g:4 * g + 4]).astype(BF16)
            s = _dot_nt(qg, res[0][:, g * LANES:(g + 1) * LANES]) + bc_ref[g]
            e = jnp.exp(s - jnp.max(s, axis=-1, keepdims=True))
            pc = (e / jnp.maximum(jnp.sum(e, axis=-1, keepdims=True), 1e-30)).astype(BF16)
            oc_ref[g] = _dot(pc, res[1][:, g * LANES:(g + 1) * LANES])
            imp = jnp.sum(_dot(pc, c2s_ref[...])[0:R_B], axis=0, keepdims=True)
            blk = lax.broadcasted_iota(jnp.int32, (R_B, ns_pad), 1)
            cur = past // SEL_BLOCK
            forced = (blk == 0) | (blk == cur) | (blk == cur - 1)
            score = jnp.where(blk <= cur, jnp.where(forced, FORCE, jnp.broadcast_to(imp, (R_B, ns_pad))), NEG)
            idx_ref[g] = _top_indices(score, SEL_TOPN)


def _ncmp(page_table, nsa_view, layer, qb, new_cmp, bias_b, pe_flat, w1bd, w2bd):
    DB, NP = page_table.shape
    past = NP * PAGE
    nblk = past // CMP_STRIDE
    ns = past // SEL_BLOCK + 1
    ns_pad = -(-ns // LANES) * LANES
    i = np.arange(nblk)
    dist = past - (i * CMP_STRIDE + CMP_LEN - 1)
    bc = _head_rows_b(jnp.where(jnp.asarray(dist >= 0), _bias_of_dist(bias_b, dist), NEG))
    ci = i[:, None] * CMP_STRIDE
    lo = np.arange(ns_pad)[None, :] * SEL_BLOCK
    c2s = jnp.asarray(((ci < lo + SEL_BLOCK) & (ci + CMP_LEN > lo) & (lo < ns * SEL_BLOCK)).astype(np.float32))
    per_b = lambda *blk: pl.BlockSpec((None,) + blk, lambda b, c, pt: (b,) + (0,) * len(blk))
    const = lambda *blk: pl.BlockSpec(blk, lambda b, c, pt: (0,) * len(blk), pipeline_mode=pl.Buffered(1))
    return pl.pallas_call(
        functools.partial(_ncmp_kernel, past=past),
        grid_spec=pltpu.PrefetchScalarGridSpec(
            num_scalar_prefetch=1,
            grid=(DB, NP // PAGES_PER_STEP),
            in_specs=[per_b(H_B // 2, LANES), per_b(2, 1, LANES), const(KV_B, SROWS, nblk),
                      const(2, 1, CMP_K), const(2, CMP_K, 2 * CMP_HID), const(2, 2 * CMP_HID, 2 * LANES),
                      const(nblk, ns_pad)]
            + _page_specs((None, None, 2, KV_B, HD, PAGE), 0, layer, 0),
            out_specs=[per_b(KV_B, SROWS, LANES), per_b(KV_B, R_B, LANES)],
            scratch_shapes=[pltpu.VMEM((past + 2 * CMP_STRIDE, LANES), F32),
                            pltpu.VMEM((past + 2 * CMP_STRIDE, LANES), F32),
                            pltpu.VMEM((nblk, CMP_K), BF16)],
        ),
        out_shape=[jax.ShapeDtypeStruct((DB, KV_B, SROWS, LANES), F32),
                   jax.ShapeDtypeStruct((DB, KV_B, R_B, LANES), jnp.int32)],
        compiler_params=_cparams(("parallel", "arbitrary")),
        name="ncmp",
    )(page_table, qb, new_cmp, bc, pe_flat, w1bd, w2bd, c2s.astype(BF16), *([nsa_view] * PAGES_PER_STEP))


def _dup_t(x):
    return jnp.concatenate([x, x], axis=0).astype(BF16)


def _nsel_kernel(pt_ref, idx_ref, q_ref, gb_ref, oc_ref, new_ref, newcol_ref, bfar_ref, blast_ref, b0_ref,
                 bw_ref, win_ref, ex_ref, *rest, n_pages):
    sel_pages = rest[:KV_B * SEL_TOPN]
    o_ref, wout_ref = rest[KV_B * SEL_TOPN:]
    b = pl.program_id(0)
    lane = lax.broadcasted_iota(jnp.int32, (SROWS, LANES), 1)
    lane4 = lax.broadcasted_iota(jnp.int32, (4, LANES), 1)
    g8 = jnp.broadcast_to(gb_ref[...], (8, LANES))
    g_hi = g8.astype(BF16)
    g_lo = (g8 - g_hi.astype(F32)).astype(BF16)
    gexp = [_dot(g_hi, ex_ref[br]) + _dot(g_lo, ex_ref[br]) for br in range(3)]

    def unstack(o):
        return jnp.where(lane4 < HD, o[0:4], o[4:8])

    def rounded(x):
        return x.astype(BF16).astype(F32)

    for g in range(KV_B):
        qg = _stack_pairs(q_ref[4 * g:4 * g + 4])
        qb = qg.astype(BF16)
        qr = rounded(qg)

        def self_score(k):
            return jnp.sum(qr * rounded(new_ref[k:k + 1]), axis=-1, keepdims=True) + b0_ref[g][:, 0:1]

        s_self = self_score(g)
        scores = []
        for t in range(SEL_TOPN):
            blk = idx_ref[b, g * SEL_TOPN + t]
            page = sel_pages[g * SEL_TOPN + t]
            s = _dot(qb, _dup_t(page[0, g]))
            s = s + jnp.where(blk // 2 == n_pages - 1, blast_ref[g], bfar_ref[g])
            ok = (blk < 2 * n_pages) & ((lane // SEL_BLOCK) == blk % 2)
            scores.append(jnp.where(ok, s, NEG))
        m = s_self
        for s in scores:
            m = jnp.maximum(m, jnp.max(s, axis=-1, keepdims=True))
        e_self = jnp.exp(s_self - m)
        l = e_self
        acc = rounded(e_self) * rounded(new_ref[4 + g:5 + g])
        for t, s in enumerate(scores):
            e = jnp.exp(s - m)
            l = l + jnp.sum(e, axis=-1, keepdims=True)
            acc = acc + _dot_nt(e.astype(BF16), _dup_t(sel_pages[g * SEL_TOPN + t][1, g]))
        o_s = unstack(acc / jnp.maximum(l, 1e-30))

        s_self = self_score(2 + g)
        s = _dot(qb, _dup_t(win_ref[0, g])) + bw_ref[g]
        m = jnp.maximum(s_self, jnp.max(s, axis=-1, keepdims=True))
        e = jnp.exp(s - m)
        e_self = jnp.exp(s_self - m)
        l = e_self + jnp.sum(e, axis=-1, keepdims=True)
        acc = rounded(e_self) * rounded(new_ref[6 + g:7 + g]) + _dot_nt(e.astype(BF16), _dup_t(win_ref[1, g]))
        o_w = unstack(acc / jnp.maximum(l, 1e-30))
        o_c = unstack(oc_ref[g])

        for p in range(4):
            c0 = (4 * g + p) * LANES
            o_ref[:, c0:c0 + LANES] = (gexp[0][0:1, c0:c0 + LANES] * o_c[p:p + 1]
                                       + gexp[1][0:1, c0:c0 + LANES] * o_s[p:p + 1]
                                       + gexp[2][0:1, c0:c0 + LANES] * o_w[p:p + 1])

    n_win = win_ref.shape[3]
    wl = lax.broadcasted_iota(jnp.int32, (HD, n_win), 1)
    for kv in range(2):
        for g in range(KV_B):
            r0 = (kv * KV_B + g) * HD
            shifted = pltpu.roll(win_ref[kv, g], n_win - 1, axis=1)
            wout_ref[kv, g] = jnp.where(wl == n_win - 1, newcol_ref[r0:r0 + HD, :], shifted)


def _nsel(page_table, idx, nsa_view, win_view, layer, qb, gates, oc, new_rows, new_col, bias_b):
    DB, NP = page_table.shape
    n_win = win_view.shape[-1]
    ones = jnp.ones((1, LANES), F32)
    bfar = _head_rows_b(bias_b[N_BUCKETS - 1][:, None] * ones)
    blast = _head_rows_b(_bias_of_dist(bias_b, PAGE - np.arange(PAGE)))
    b0 = _head_rows_b(bias_b[0][:, None] * ones)
    bw = _head_rows_b(_bias_of_dist(bias_b, n_win - np.arange(n_win)))
    k = np.arange(LANES)[:, None]
    col = np.arange(H_B * HD)[None, :]
    ex = jnp.asarray(np.stack([(k == (col // HD) * 3 + br) for br in range(3)]).astype(np.float32)).astype(BF16)
    per_b = lambda *blk: pl.BlockSpec((None,) + blk, lambda b, pt, ix: (b,) + (0,) * len(blk))
    const = lambda *blk: pl.BlockSpec(blk, lambda b, pt, ix: (0,) * len(blk))

    def sel_spec(g, t):
        def index_map(b, pt, ix):
            return (layer, pt[b, jnp.clip(ix[b, g * SEL_TOPN + t] // 2, 0, NP - 1)], 1, 0, 0, 0)
        return pl.BlockSpec((None, None, 2, KV_B, HD, PAGE), index_map)

    return pl.pallas_call(
        functools.partial(_nsel_kernel, n_pages=NP),
        grid_spec=pltpu.PrefetchScalarGridSpec(
            num_scalar_prefetch=2,
            grid=(DB,),
            in_specs=[per_b(H_B // 2, LANES), per_b(1, LANES), per_b(KV_B, SROWS, LANES), per_b(8, LANES),
                      per_b(4 * HD, 1), const(KV_B, SROWS, LANES), const(KV_B, SROWS, LANES),
                      const(KV_B, SROWS, LANES), const(KV_B, SROWS, n_win),
                      pl.BlockSpec((None, None, 2, KV_B, HD, n_win), lambda b, pt, ix: (layer, b, 0, 0, 0, 0)),
                      const(3, LANES, H_B * HD)]
            + [sel_spec(g, t) for g in range(KV_B) for t in range(SEL_TOPN)],
            out_specs=[per_b(1, H_B * HD), per_b(2, KV_B, HD, n_win)],
        ),
        out_shape=[jax.ShapeDtypeStruct((DB, 1, H_B * HD), F32),
                   jax.ShapeDtypeStruct((DB, 2, KV_B, HD, n_win), F32)],
        compiler_params=_cparams(("parallel",)),
        name="nsel",
    )(page_table, idx, qb, gates, oc, new_rows, new_col, bfar, blast, b0, bw, win_view, ex,
      *([nsa_view] * (KV_B * SEL_TOPN)))


def _row_tile(n, cap):
    return n if n <= cap else cap


def _ff_tile(F):
    return F // 2 if (F > 1408 and (F // 2) % LANES == 0) else F


def kernel(x_prompt, x_sample, cache_attn_kv, cache_nsa_kv, state_win_kv, page_table, rel_bias, norm_mix, w_in, w_a, w_b, w_o, diff_lambda, diff_subln, cmp_pe, cmp_w1, cmp_w2, norm_ffn, ffn_w1, ffn_w3, ffn_w2, moe_router, moe_w1, moe_w3, moe_w2, norm_final):
    B, T, D = x_prompt.shape
    DB, dec_t, _ = x_sample.shape
    assert dec_t == 1
    depth = w_in.shape[0]
    n_pool = cache_attn_kv.shape[1]
    n_win = state_win_kv.shape[2]
    bias_a, bias_b = rel_bias[:, :H_A], rel_bias[:, H_A:]
    attn_view = cache_attn_kv.reshape(depth, n_pool, PAGE * 2 * KV_A, LANES)
    nsa_view = jnp.transpose(cache_nsa_kv, (0, 1, 3, 4, 5, 2))
    win_view = jnp.transpose(state_win_kv, (0, 1, 3, 4, 5, 2))

    xp = x_prompt.reshape(B * T, D)
    xs = x_sample.reshape(DB, D)
    tm = _row_tile(T, 512)
    att_t = _row_tile(T, ATT_T)
    outs = [[] for _ in range(6)]
    for l in range(depth):
        lam_init = 0.8 - 0.6 * math.exp(-0.3 * l)
        last = l == depth - 1
        w_row, w_t, w_gm = _prep_in_weights(w_in[l])
        wa, wb, wo = w_a[l].astype(BF16), w_b[l].astype(BF16), w_o[l].astype(BF16)
        pe_flat, w1bd, w2bd = _prep_cmp_weights(cmp_pe[l], cmp_w1[l], cmp_w2[l])

        qa4, kva, va, qb4, cmp_rows, vdup, gb, kat, nsat, wint, kdupt = _proj(
            xp.reshape(B, T, D), norm_mix[l], w_row, w_t, tm)
        a = _dattn(qa4, kat, va, bias_a, diff_lambda[l], diff_subln[l], lam_init, att_t)
        kcv = _compress(cmp_rows, B, T, pe_flat, w1bd, w2bd)
        ob = _nsa(qb4, gb, kcv, kdupt, vdup, bias_b, NSA_TQ, NSA_TK)
        xp = _merge(xp, norm_mix[l], a, ob, w_gm, wa, wb, wo, tm)
        outs[0].append(kva.reshape(B, T, 2, KV_A, 2 * HD))
        outs[2].append(jnp.transpose(nsat.reshape(B, 4, KV_B, HD, T), (0, 4, 1, 2, 3)))
        wlen = min(WINDOW, T)
        outs[4].append(jnp.transpose(wint[:, :, T - wlen:].reshape(B, 2, KV_B, HD, wlen), (0, 4, 1, 2, 3)))

        qa_s, kva_s, _, qb_s, cmp_s, vdup_s, gb_s, _, nsat_s, wint_s, kdupt_s = _proj(
            xs.reshape(1, DB, D), norm_mix[l], w_row, w_t, DB)
        kva_s = kva_s.reshape(DB, 2, KV_A, 2 * HD)
        q_a = jnp.transpose(qa_s[0], (1, 0, 2)).astype(F32)
        q_b = jnp.transpose(qb_s[0], (1, 0, 2)).astype(F32)
        a_s = _adec(page_table, attn_view, l, q_a, kva_s[:, 0], kva_s[:, 1], bias_a, diff_lambda[l],
                    diff_subln[l], lam_init)
        oc, idx = _ncmp(page_table, nsa_view, l, q_b, cmp_s.reshape(DB, 2, 1, LANES), bias_b, pe_flat, w1bd, w2bd)
        new_rows = jnp.concatenate([jnp.transpose(kdupt_s[0], (2, 0, 1)), jnp.transpose(vdup_s[0], (1, 0, 2))],
                                   axis=1).astype(F32)
        new_col = jnp.transpose(wint_s[0]).reshape(DB, 4 * HD, 1)
        ob_s, wout = _nsel(page_table, idx[:, :, 0, :SEL_TOPN].reshape(DB, KV_B * SEL_TOPN), nsa_view, win_view, l, q_b,
                           gb_s.reshape(DB, 1, LANES), oc, new_rows, new_col, bias_b)
        xs = _merge(xs, norm_mix[l], a_s.reshape(DB, H_A * 2 * HD).astype(BF16),
                    ob_s.reshape(DB, H_B * HD).astype(BF16), w_gm, wa, wb, wo, DB)
        outs[1].append(kva_s.reshape(DB, 1, 2, KV_A, 2 * HD))
        outs[3].append(jnp.transpose(nsat_s[0]).reshape(DB, 1, 4, KV_B, HD))
        outs[5].append(jnp.transpose(wout, (0, 4, 1, 2, 3)))

        e = l // 2
        if l % 2 == 0:
            w1, w3, w2 = ffn_w1[e].astype(BF16), ffn_w3[e].astype(BF16), ffn_w2[e].astype(BF16)
            tf = _ff_tile(w1.shape[1])
            xp = _ffn(xp, norm_ffn[l], w1, w3, w2, norm_final, last, tm, tf)
            xs = _ffn(xs, norm_ffn[l], w1, w3, w2, norm_final, last, DB, tf)
        else:
            w1, w3, w2 = moe_w1[e].astype(BF16), moe_w3[e].astype(BF16), moe_w2[e].astype(BF16)
            w_r = jnp.pad(moe_router[e], ((0, 0), (0, LANES - N_EXPERTS))).astype(BF16)
            tf = _ff_tile(w1.shape[2])
            xp = _moe(xp, norm_ffn[l], w_r, w1, w3, w2, norm_final, last, _row_tile(B * T, 1024), tf)
            xs = _moe(xs, norm_ffn[l], w_r, w1, w3, w2, norm_final, last, DB, tf)

    return (xp.reshape(B, T, D), xs.reshape(DB, 1, D)) + tuple(jnp.stack(o) for o in outs)
```

```python
import functools
import math

import numpy as np
import jax
import jax.numpy as jnp
from jax import lax
from jax.experimental import pallas as pl
from jax.experimental.pallas import tpu as pltpu

F32 = jnp.float32
BF16 = jnp.bfloat16

HD = 64
H_A = 8
KV_A = 4
H_B = 16
KV_B = 2
R_B = H_B // KV_B
CMP_LEN = 32
CMP_STRIDE = 16
CMP_HID = 128
SEL_BLOCK = 64
SEL_TOPN = 16
WINDOW = 512
N_BUCKETS = 32
MAX_DIST = 128
N_EXPERTS = 8
PAGE = 128
EPS = 1e-6
NEG = -1e30
FORCE = 1e9
REMOVED = -3e38

LANES = 128
VMEM_LIMIT = 56 * 1024 * 1024

ATT_T = 512
NSA_TQ = 128
NSA_TK = 512
WIN_W = WINDOW + NSA_TQ
PAGES_PER_STEP = 8

N_ROW = 3968
N_T = 1792


def _cparams(sem):
    return pltpu.CompilerParams(dimension_semantics=sem, vmem_limit_bytes=VMEM_LIMIT)


def _resident(block, index_map):
    return pl.BlockSpec(block, index_map, pipeline_mode=pl.Buffered(1))


def _rms(x, g):
    xf = x.astype(F32)
    return xf * lax.rsqrt(jnp.mean(xf * xf, axis=-1, keepdims=True) + EPS) * g


def _dot(a, b):
    return jnp.dot(a, b, preferred_element_type=F32)


def _dot_nt(a, b):
    return lax.dot_general(a, b, (((1,), (1,)), ((), ())), preferred_element_type=F32)


def _bucket_np(d):
    d = np.maximum(np.asarray(d, np.int64), 0)
    exact = N_BUCKETS // 2
    df = np.maximum(d, 1).astype(np.float32)
    far = exact + (np.log(df / np.float32(exact)) / np.float32(math.log(MAX_DIST / exact))
                   * np.float32(N_BUCKETS - exact)).astype(np.int64)
    return np.where(d < exact, d, np.minimum(far, N_BUCKETS - 1))


FAR_DIST = 128
assert np.all(_bucket_np(np.arange(FAR_DIST, 1 << 16)) == N_BUCKETS - 1)


def _bias_of_dist(table, dist):
    idx = jnp.asarray(_bucket_np(dist).astype(np.int32))
    return jnp.moveaxis(jnp.take(table, idx, axis=0), -1, 0)


def _proj_kernel(x_ref, g_ref, wr_ref, wt_ref, qa_ref, kva_ref, va_ref, qb_ref, cmp_ref, vdup_ref, gb_ref,
                 kat_ref, nsat_ref, wint_ref, kdupt_ref):
    tm = x_ref.shape[0]
    hb = _rms(x_ref[...], g_ref[...]).astype(BF16)

    def cols(a, b):
        return _dot(hb, wr_ref[:, a:b])

    z = cols(0, 1024)
    for h in range(H_A):
        qa_ref[h] = z[:, h * LANES:(h + 1) * LANES].astype(BF16)
    z = cols(1024, 2048)
    for c in range(2 * KV_A):
        kva_ref[pl.ds(c, tm, stride=2 * KV_A), :] = z[:, c * LANES:(c + 1) * LANES]
    va_ref[...] = z[:, 512:1024].astype(BF16)
    z = cols(2048, 3072)
    for p in range(H_B // 2):
        qb_ref[p] = z[:, p * LANES:(p + 1) * LANES].astype(BF16)
    cmp_ref[...] = cols(3072, 3328)
    z = cols(3328, 3840)
    for k in range(4):
        vdup_ref[k] = z[:, k * LANES:(k + 1) * LANES].astype(BF16)
    gb_ref[...] = jax.nn.sigmoid(cols(3840, N_ROW))

    def rows(a, b):
        return _dot_nt(wt_ref[a:b, :], hb)

    z = rows(0, 512)
    for g in range(KV_A):
        kat_ref[g] = z[g * LANES:(g + 1) * LANES].astype(BF16)
    nsat_ref[...] = rows(512, 1024)
    wint_ref[...] = rows(1024, 1280)
    z = rows(1280, N_T)
    for k in range(4):
        kdupt_ref[k] = z[k * LANES:(k + 1) * LANES].astype(BF16)


def _proj(x, gain, w_row, w_t, tm):
    Bp, T, D = x.shape
    nt = T // tm
    n = Bp * T
    row = lambda b, i: (b * nt + i, 0)
    out_shape = (
        jax.ShapeDtypeStruct((Bp, H_A, T, LANES), BF16),
        jax.ShapeDtypeStruct((n * 2 * KV_A, LANES), F32),
        jax.ShapeDtypeStruct((n, KV_A * 2 * HD), BF16),
        jax.ShapeDtypeStruct((Bp, H_B // 2, T, LANES), BF16),
        jax.ShapeDtypeStruct((n, 4 * HD), F32),
        jax.ShapeDtypeStruct((Bp, 4, T, LANES), BF16),
        jax.ShapeDtypeStruct((n, LANES), F32),
        jax.ShapeDtypeStruct((Bp, KV_A, LANES, T), BF16),
        jax.ShapeDtypeStruct((Bp, 512, T), F32),
        jax.ShapeDtypeStruct((Bp, 256, T), F32),
        jax.ShapeDtypeStruct((Bp, 4, LANES, T), BF16),
    )
    out_specs = (
        pl.BlockSpec((None, H_A, tm, LANES), lambda b, i: (b, 0, i, 0)),
        pl.BlockSpec((tm * 2 * KV_A, LANES), row),
        pl.BlockSpec((tm, 512), row),
        pl.BlockSpec((None, H_B // 2, tm, LANES), lambda b, i: (b, 0, i, 0)),
        pl.BlockSpec((tm, 256), row),
        pl.BlockSpec((None, 4, tm, LANES), lambda b, i: (b, 0, i, 0)),
        pl.BlockSpec((tm, LANES), row),
        pl.BlockSpec((None, KV_A, LANES, tm), lambda b, i: (b, 0, 0, i)),
        pl.BlockSpec((None, 512, tm), lambda b, i: (b, 0, i)),
        pl.BlockSpec((None, 256, tm), lambda b, i: (b, 0, i)),
        pl.BlockSpec((None, 4, LANES, tm), lambda b, i: (b, 0, 0, i)),
    )
    return pl.pallas_call(
        _proj_kernel,
        grid=(Bp, nt),
        in_specs=[
            pl.BlockSpec((None, tm, D), lambda b, i: (b, i, 0)),
            pl.BlockSpec((1, D), lambda b, i: (0, 0)),
            _resident((D, N_ROW), lambda b, i: (0, 0)),
            _resident((N_T, D), lambda b, i: (0, 0)),
        ],
        out_specs=out_specs,
        out_shape=out_shape,
        compiler_params=_cparams(("parallel", "parallel")),
        name="proj",
    )(x, gain.reshape(1, D), w_row, w_t)


def _prep_in_weights(w_in_l):
    wt = w_in_l.T
    D = wt.shape[1]
    scale = HD ** -0.5
    qa, ka, va = wt[0:1024] * scale, wt[1024:1536], wt[1536:2048]
    qb = wt[2048:3072] * scale
    kvb = wt[3072:3840].reshape(6, KV_B, HD, D)
    gb = wt[3840:3888]
    gm = wt[3888:]
    dup = lambda w: jnp.concatenate([w, w], axis=0)
    vdup = jnp.concatenate([dup(kvb[k, g]) for k in (3, 5) for g in range(KV_B)], axis=0)
    kdup = jnp.concatenate([dup(kvb[k, g]) for k in (2, 4) for g in range(KV_B)], axis=0)
    w_row_t = jnp.concatenate(
        [qa, ka, va, qb, kvb[0:2].reshape(256, D), vdup, gb, jnp.zeros((LANES - gb.shape[0], D), F32)], axis=0)
    w_t = jnp.concatenate([ka, kvb[0:4].reshape(512, D), kvb[4:6].reshape(256, D), kdup], axis=0)
    return w_row_t.T.astype(BF16), w_t.astype(BF16), gm.T.astype(BF16)


def _softmax_step(s, v, m_ref, l_ref, acc_ref, idx):
    m_old = m_ref[idx]
    m_new = jnp.maximum(m_old, jnp.max(s, axis=-1, keepdims=True))
    p = jnp.exp(s - pltpu.repeat(m_new, s.shape[1] // LANES, axis=1))
    alpha = jnp.exp(m_old - m_new)
    l_ref[idx] = alpha * l_ref[idx] + jnp.sum(p, axis=-1, keepdims=True)
    acc_ref[idx] = alpha * acc_ref[idx] + _dot(p.astype(BF16), v)
    m_ref[idx] = m_new


def _toeplitz(u, n):
    H = u.shape[0]
    wp = jnp.pad(u[:, ::-1], ((0, 0), (0, 1)))
    r = jnp.broadcast_to(wp[:, None, :], (H, n, 2 * n)).reshape(H, 2 * n * n)[:, :n * (2 * n - 1)]
    return r.reshape(H, n, 2 * n - 1)[:, :, n - 1:]


def _near_tiles(table, t, masked):
    far = table[N_BUCKETS - 1][:, None]
    k = np.arange(2 * t - 1)
    d0 = jnp.where(jnp.asarray(k >= t - 1), _bias_of_dist(table, k - (t - 1)) - far, NEG if masked else 0.0)
    d1 = _bias_of_dist(table, k + 1) - far
    return jnp.stack([_toeplitz(d0, t), _toeplitz(d1, t)], axis=1)


def _diff_lambda(lp, lam_init):
    a = jnp.sum(lp[0:1] * lp[1:2], axis=-1, keepdims=True)
    b = jnp.sum(lp[2:3] * lp[3:4], axis=-1, keepdims=True)
    return jnp.exp(a) - jnp.exp(b) + lam_init


def _diff_finish(o0, o1, lp, subln, lam_init):
    o = o0 - _diff_lambda(lp, lam_init) * o1
    return _rms(o, subln) * (1.0 - lam_init)


def _dattn_kernel(q_ref, kt_ref, v_ref, bias_ref, lam_ref, sub_ref, o_ref, m_ref, l_ref, acc_ref, *, tq, lam_init):
    qi = pl.program_id(2)
    lane = lax.broadcasted_iota(jnp.int32, (tq, LANES), 1)
    m_ref[...] = jnp.full(m_ref.shape, NEG, F32)
    l_ref[...] = jnp.zeros(l_ref.shape, F32)
    acc_ref[...] = jnp.zeros(acc_ref.shape, F32)

    def step(j, kind):
        start = pl.multiple_of(j * tq, tq)
        kt = kt_ref[:, pl.ds(start, tq)]
        v = v_ref[pl.ds(start, tq), :]
        for r in range(2):
            q = q_ref[r]
            zero = jnp.zeros_like(q)
            for c in range(2):
                s = _dot(jnp.where(lane < HD if c == 0 else lane >= HD, q, zero), kt)
                if kind is not None:
                    s = s + bias_ref[r, kind]
                _softmax_step(s, v, m_ref, l_ref, acc_ref, (c, slice(r * tq, (r + 1) * tq)))

    def far_body(j, carry):
        step(j, None)
        return carry

    lax.fori_loop(0, jnp.maximum(qi - 1, 0), far_body, 0)

    @pl.when(qi >= 1)
    def _():
        step(qi - 1, 1)

    step(qi, 0)

    o0 = acc_ref[0] / jnp.maximum(l_ref[0], 1e-30)
    o1 = acc_ref[1] / jnp.maximum(l_ref[1], 1e-30)
    y = _diff_finish(o0, o1, lam_ref[...], sub_ref[...], lam_init).astype(BF16)
    o_ref[:, 0:LANES] = y[:tq]
    o_ref[:, LANES:2 * LANES] = y[tq:]


def _dattn(qa4, kat, va, bias_a, lam_p, subln, lam_init, tq):
    B, _, T, _ = qa4.shape
    nq = T // tq
    tiles = _near_tiles(bias_a, tq, masked=True)
    return pl.pallas_call(
        functools.partial(_dattn_kernel, tq=tq, lam_init=lam_init),
        grid=(B, KV_A, nq),
        in_specs=[
            pl.BlockSpec((None, 2, tq, LANES), lambda b, g, i: (b, g, i, 0)),
            pl.BlockSpec((None, None, LANES, T), lambda b, g, i: (b, g, 0, 0)),
            pl.BlockSpec((T, LANES), lambda b, g, i: (b, g)),
            pl.BlockSpec((2, 2, tq, tq), lambda b, g, i: (g, 0, 0, 0)),
            pl.BlockSpec((4, HD), lambda b, g, i: (0, 0)),
            pl.BlockSpec((1, 2 * HD), lambda b, g, i: (0, 0)),
        ],
        out_specs=pl.BlockSpec((tq, 2 * LANES), lambda b, g, i: (b * nq + i, g)),
        out_shape=jax.ShapeDtypeStruct((B * T, H_A * 2 * HD), BF16),
        scratch_shapes=[pltpu.VMEM((2, 2 * tq, LANES), F32)] * 3,
        compiler_params=_cparams(("parallel", "parallel", "arbitrary")),
        name="dattn",
    )(qa4, kat, va, tiles, lam_p, subln.reshape(1, 2 * HD))


CMP_K = CMP_LEN * KV_B * HD


def _compress_core(xs_ref, nblk, pe_ref, w1_ref, w2_ref, p_ref):
    for j in range(CMP_LEN):
        xj = xs_ref[pl.ds(j, nblk, stride=CMP_STRIDE), :]
        p_ref[:, j * LANES:(j + 1) * LANES] = (xj + pe_ref[:, j * LANES:(j + 1) * LANES]).astype(BF16)
    hid = jax.nn.gelu(_dot(p_ref[...], w1_ref[...]))
    return _dot(hid.astype(BF16), w2_ref[...])


def _compress_kernel(x_ref, pe_ref, w1_ref, w2_ref, o_ref, xs_ref, p_ref, *, T):
    xs_ref[0:T, :] = x_ref[...]
    xs_ref[T:T + CMP_STRIDE, :] = jnp.zeros((CMP_STRIDE, LANES), F32)
    res = _compress_core(xs_ref, T // CMP_STRIDE, pe_ref, w1_ref, w2_ref, p_ref)
    o_ref[0] = res[:, 0:LANES].astype(BF16)
    o_ref[1] = res[:, LANES:2 * LANES].astype(BF16)


def _prep_cmp_weights(pe, w1, w2):
    pe_flat = jnp.broadcast_to(pe[:, :, None, :], (2, CMP_LEN, KV_B, HD)).reshape(2, 1, CMP_K)
    w1r = w1.reshape(2, CMP_LEN, HD, CMP_HID)
    w1bd = jnp.zeros((2, CMP_LEN, KV_B, HD, KV_B, CMP_HID), F32)
    w2bd = jnp.zeros((2, KV_B, CMP_HID, KV_B, 2, HD), F32)
    for g in range(KV_B):
        w1bd = w1bd.at[:, :, g, :, g, :].set(w1r)
        w2bd = w2bd.at[:, g, :, g, :, :].set(jnp.broadcast_to(w2[:, :, None, :], (2, CMP_HID, 2, HD)))
    return (pe_flat, w1bd.reshape(2, CMP_K, KV_B * CMP_HID).astype(BF16),
            w2bd.reshape(2, KV_B * CMP_HID, KV_B * 2 * HD).astype(BF16))


def _compress(cmp_rows, B, T, pe_flat, w1bd, w2bd):
    nblk = T // CMP_STRIDE
    return pl.pallas_call(
        functools.partial(_compress_kernel, T=T),
        grid=(B, 2),
        in_specs=[
            pl.BlockSpec((T, LANES), lambda b, k: (b, k)),
            pl.BlockSpec((None, 1, CMP_K), lambda b, k: (k, 0, 0)),
            pl.BlockSpec((None, CMP_K, 2 * CMP_HID), lambda b, k: (k, 0, 0)),
            pl.BlockSpec((None, 2 * CMP_HID, 2 * LANES), lambda b, k: (k, 0, 0)),
        ],
        out_specs=pl.BlockSpec((None, None, KV_B, nblk, LANES), lambda b, k: (b, k, 0, 0, 0)),
        out_shape=jax.ShapeDtypeStruct((B, 2, KV_B, nblk, LANES), BF16),
        scratch_shapes=[pltpu.VMEM((T + CMP_STRIDE, LANES), F32), pltpu.VMEM((nblk, CMP_K), BF16)],
        compiler_params=_cparams(("parallel", "parallel")),
        name="compress",
    )(cmp_rows, pe_flat, w1bd, w2bd)


def _group_head(g, rb):
    return R_B * g + 2 * (rb % 4) + rb // 4


def _stack_query(q4):
    lane = lax.broadcasted_iota(jnp.int32, q4.shape, 1)
    zero = jnp.zeros_like(q4)
    return jnp.concatenate([jnp.where(lane < HD, q4, zero), jnp.where(lane >= HD, q4, zero)], axis=0)


def _top_blocks(score, n):
    lanef = lax.broadcasted_iota(jnp.int32, score.shape, 1).astype(F32)
    width = float(score.shape[1])

    def body(_, carry):
        sc, sel = carry
        mx = jnp.max(sc, axis=1, keepdims=True)
        idx = jnp.min(jnp.where(sc == mx, lanef, width), axis=1, keepdims=True)
        hit = lanef == idx
        return jnp.where(hit, REMOVED, sc), jnp.where(hit, 1.0, sel)

    return lax.fori_loop(0, n, body, (score, jnp.zeros_like(score)))[1]


def _nsa_kernel(q_ref, gb_ref, kcv_ref, kselt_ref, kwint_ref, vsel_ref, vwin_ref, d_ref, ac_ref,
                c2s_ref, ex_ref, o_ref, s_ref, m_ref, l_ref, acc_ref, *, tq, tk, T):
    qi = pl.program_id(1)
    p0 = qi * tq
    nc = kcv_ref.shape[2]
    rows = R_B * tq
    half = rows // 2
    qpos = p0 + lax.broadcasted_iota(jnp.int32, (rows, 1), 0) % tq
    qpos_h = qpos[:half]
    lane4 = lax.broadcasted_iota(jnp.int32, (half, LANES), 1)

    gates = gb_ref[...]
    g_hi = gates.astype(BF16)
    g_lo = (gates - g_hi.astype(F32)).astype(BF16)
    gexp = [_dot(g_hi, ex_ref[br]) + _dot(g_lo, ex_ref[br]) for br in range(3)]

    def add_tile(hf, kind, off):
        for k in range(4):
            s_ref[k * tq:(k + 1) * tq, pl.ds(off, tq)] += d_ref[_group_head(g, 4 * hf + k), kind]

    for g in range(KV_B):
        Q = _stack_query(q_ref[4 * g:4 * g + 4].reshape(4 * tq, LANES))

        st = (tq // CMP_STRIDE) * qi - LANES // 2
        wi = lax.broadcasted_iota(jnp.int32, (LANES, nc), 0)
        ci = lax.broadcasted_iota(jnp.int32, (LANES, nc), 1)
        place = jnp.where(ci == wi + st, 1.0, 0.0).astype(BF16)
        s = _dot_nt(Q, kcv_ref[0, g]) + _dot(ac_ref[0, g], place) + _dot(ac_ref[1, g], place)
        cend = lax.broadcasted_iota(jnp.int32, (1, nc), 1) * CMP_STRIDE + (CMP_LEN - 1)
        mask = cend <= qpos
        s = jnp.where(mask, s, NEG)
        e = jnp.where(mask, jnp.exp(s - jnp.max(s, axis=-1, keepdims=True)), 0.0)
        pc = (e / jnp.maximum(jnp.sum(e, axis=-1, keepdims=True), 1e-30)).astype(BF16)
        oc = _dot(pc, kcv_ref[1, g])
        o_c = jnp.where(lane4 < HD, oc[:half], oc[half:])
        imp = jnp.sum(_dot(pc, c2s_ref[...]).reshape(R_B, tq, LANES), axis=0)

        blk = lax.broadcasted_iota(jnp.int32, (tq, LANES), 1)
        cur = (p0 + lax.broadcasted_iota(jnp.int32, (tq, 1), 0)) // SEL_BLOCK
        forced = (blk == 0) | (blk == cur) | (blk == cur - 1)
        score = jnp.where(blk <= cur, jnp.where(forced, FORCE, imp), NEG)
        sel = _top_blocks(score, SEL_TOPN)
        selneg = jnp.where((sel > 0.5) & (blk <= cur), 0.0, NEG).astype(BF16)

        m_ref[...] = jnp.full(m_ref.shape, NEG, F32)
        l_ref[...] = jnp.zeros(l_ref.shape, F32)
        acc_ref[...] = jnp.zeros(acc_ref.shape, F32)
        bi = lax.broadcasted_iota(jnp.int32, (LANES, tk), 0)
        kb = lax.broadcasted_iota(jnp.int32, (LANES, tk), 1) // SEL_BLOCK

        def sel_block(j, mode):
            ks = pl.multiple_of(j * tk, tk)
            kt = kselt_ref[g, :, pl.ds(ks, tk)]
            v = vsel_ref[g, pl.ds(ks, tk), :]
            expand = jnp.where(bi - kb == j * (tk // SEL_BLOCK), 1.0, 0.0).astype(BF16)
            madd = _dot(selneg, expand)
            for hf in range(2):
                s = (_dot(Q[hf * half:(hf + 1) * half], kt).reshape(4, tq, tk) + madd[None]).reshape(half, tk)
                if mode:
                    s_ref[:, 0:tk] = s
                    if mode == 1:
                        add_tile(hf, 1, tk - tq)
                    else:
                        off0 = p0 - ks
                        add_tile(hf, 0, pl.multiple_of(off0, tq))

                        @pl.when(off0 >= tq)
                        def _():
                            add_tile(hf, 1, pl.multiple_of(off0 - tq, tq))

                    s = s_ref[:, 0:tk]
                    if mode == 2:
                        key = ks + lax.broadcasted_iota(jnp.int32, (1, tk), 1)
                        s = jnp.where(key <= qpos_h, s, NEG)
                _softmax_step(s, v, m_ref, l_ref, acc_ref, (0, slice(hf * half, (hf + 1) * half)))

        jd = p0 // tk
        at_block_start = (p0 % tk == 0) & (jd >= 1)

        def far_body(j, carry):
            sel_block(j, 0)
            return carry

        lax.fori_loop(0, jd - at_block_start.astype(jnp.int32), far_body, 0)

        @pl.when(at_block_start)
        def _():
            sel_block(jd - 1, 1)

        sel_block(jd, 2)
        o = acc_ref[0] / jnp.maximum(l_ref[0], 1e-30)
        o_s = jnp.where(lane4 < HD, o[:half], o[half:])

        ws = pl.multiple_of(jnp.maximum(p0 - WINDOW, 0), tq)
        ww = WINDOW + tq
        kt = kwint_ref[g, :, pl.ds(ws, ww)]
        v = vwin_ref[g, pl.ds(ws, ww), :]
        off0 = p0 - ws
        dist = qpos_h - (ws + lax.broadcasted_iota(jnp.int32, (1, ww), 1))
        mask = (dist >= 0) & (dist <= WINDOW)
        o_half = []
        for hf in range(2):
            s_ref[...] = _dot(Q[hf * half:(hf + 1) * half], kt)
            add_tile(hf, 0, pl.multiple_of(off0, tq))

            @pl.when(off0 >= tq)
            def _():
                add_tile(hf, 1, pl.multiple_of(off0 - tq, tq))

            s = jnp.where(mask, s_ref[...], NEG)
            e = jnp.where(mask, jnp.exp(s - jnp.max(s, axis=-1, keepdims=True)), 0.0)
            o_half.append(_dot(e.astype(BF16), v) / jnp.maximum(jnp.sum(e, axis=-1, keepdims=True), 1e-30))
        o_w = jnp.where(lane4 < HD, o_half[0], o_half[1])

        for p in range(4):
            c0 = (4 * g + p) * LANES
            r = slice(p * tq, (p + 1) * tq)
            o_ref[:, c0:c0 + LANES] = (gexp[0][:, c0:c0 + LANES] * o_c[r] + gexp[1][:, c0:c0 + LANES] * o_s[r]
                                       + gexp[2][:, c0:c0 + LANES] * o_w[r]).astype(BF16)


def _nsa_tables(bias_b, tq, nc, ns_pad):
    far = bias_b[N_BUCKETS - 1]
    i = np.arange(tq)[:, None]
    tiles = _near_tiles(bias_b, tq, masked=False)
    w = np.arange(LANES)[None, :]
    dist = i + CMP_STRIDE * (LANES // 2) - CMP_STRIDE * w - (CMP_LEN - 1)
    band = jnp.where(jnp.asarray((dist >= 0) & (dist < FAR_DIST)),
                     _bias_of_dist(bias_b, dist) - far[:, None, None], 0.0)
    order = [[_group_head(g, rb) for rb in range(R_B)] for g in range(KV_B)]
    band = jnp.stack([jnp.concatenate([band[h] for h in order[g]], axis=0) for g in range(KV_B)])
    b_hi = band.astype(BF16)
    b_lo = (band - b_hi.astype(F32)).astype(BF16)
    ac = jnp.stack([b_hi, b_lo])
    ci = np.arange(nc)[:, None] * CMP_STRIDE
    lo = np.arange(ns_pad)[None, :] * SEL_BLOCK
    c2s = jnp.asarray(((ci < lo + SEL_BLOCK) & (ci + CMP_LEN > lo)).astype(np.float32)).astype(BF16)
    k = np.arange(LANES)[:, None]
    col = np.arange(H_B * HD)[None, :]
    ex = jnp.asarray(np.stack([(k == (col // HD) * 3 + br) for br in range(3)]).astype(np.float32)).astype(BF16)
    return tiles, ac, c2s, ex


def _nsa(qb4, gb, kcv, kdupt, vdup, bias_b, tq, tk):
    B, _, T, _ = qb4.shape
    nq = T // tq
    nc = kcv.shape[3]
    assert T // SEL_BLOCK <= LANES and T >= WINDOW + tq and T % tk == 0
    tiles, ac, c2s, ex = _nsa_tables(bias_b, tq, nc, LANES)
    const = lambda *shape: pl.BlockSpec(shape, lambda b, i: (0,) * len(shape))
    return pl.pallas_call(
        functools.partial(_nsa_kernel, tq=tq, tk=tk, T=T),
        grid=(B, nq),
        in_specs=[
            pl.BlockSpec((None, H_B // 2, tq, LANES), lambda b, i: (b, 0, i, 0)),
            pl.BlockSpec((tq, LANES), lambda b, i: (b * nq + i, 0)),
            _resident((None, 2, KV_B, nc, LANES), lambda b, i: (b, 0, 0, 0, 0)),
            _resident((None, KV_B, LANES, T), lambda b, i: (b, 0, 0, 0)),
            _resident((None, KV_B, LANES, T), lambda b, i: (b, 1, 0, 0)),
            _resident((None, KV_B, T, LANES), lambda b, i: (b, 0, 0, 0)),
            _resident((None, KV_B, T, LANES), lambda b, i: (b, 1, 0, 0)),
            _resident((H_B, 2, tq, tq), lambda b, i: (0, 0, 0, 0)),
            _resident((2, KV_B, R_B * tq, LANES), lambda b, i: (0, 0, 0, 0)),
            _resident((nc, LANES), lambda b, i: (0, 0)),
            _resident((3, LANES, H_B * HD), lambda b, i: (0, 0, 0)),
        ],
        out_specs=pl.BlockSpec((tq, H_B * HD), lambda b, i: (b * nq + i, 0)),
        out_shape=jax.ShapeDtypeStruct((B * T, H_B * HD), BF16),
        scratch_shapes=[pltpu.VMEM((R_B * tq // 2, WINDOW + tq), F32)] + [pltpu.VMEM((1, R_B * tq, LANES), F32)] * 3,
        compiler_params=_cparams(("parallel", "arbitrary")),
        name="nsa",
    )(qb4, gb, kcv, kdupt, kdupt, vdup, vdup, tiles, ac, c2s, ex)


def _merge_kernel(x_ref, g_ref, a_ref, b_ref, wgm_ref, wa_ref, wb_ref, wo_ref, o_ref):
    D = x_ref.shape[1]
    x = x_ref[...]
    hb = _rms(x, g_ref[...]).astype(BF16)
    ya = jax.nn.sigmoid(_dot(hb, wgm_ref[:, 0:D])) * _dot(a_ref[...], wa_ref[...])
    yb = jax.nn.sigmoid(_dot(hb, wgm_ref[:, D:2 * D])) * _dot(b_ref[...], wb_ref[...])
    o_ref[...] = x + _dot((ya + yb).astype(BF16), wo_ref[...])


def _merge(x2, gain, a, ob, w_gm, w_a, w_b, w_o, tm):
    n, D = x2.shape
    row = lambda i: (i, 0)
    return pl.pallas_call(
        _merge_kernel,
        grid=(n // tm,),
        in_specs=[
            pl.BlockSpec((tm, D), row),
            pl.BlockSpec((1, D), lambda i: (0, 0)),
            pl.BlockSpec((tm, a.shape[1]), row),
            pl.BlockSpec((tm, ob.shape[1]), row),
            _resident(w_gm.shape, lambda i: (0, 0)),
            _resident(w_a.shape, lambda i: (0, 0)),
            _resident(w_b.shape, lambda i: (0, 0)),
            _resident(w_o.shape, lambda i: (0, 0)),
        ],
        out_specs=pl.BlockSpec((tm, D), row),
        out_shape=jax.ShapeDtypeStruct((n, D), F32),
        compiler_params=_cparams(("parallel",)),
        name="merge",
    )(x2, gain.reshape(1, D), a, ob, w_gm, w_a, w_b, w_o)


def _swiglu_chunk(hb, w1_ref, w3_ref, w2_ref):
    t = jax.nn.silu(_dot(hb, w1_ref[...])) * _dot(hb, w3_ref[...])
    return _dot(t.astype(BF16), w2_ref[...])


def _ffn_kernel(x_ref, g_ref, w1_ref, w3_ref, w2_ref, gf_ref, o_ref, hb_ref, acc_ref, *, final_norm):
    f = pl.program_id(1)

    @pl.when(f == 0)
    def _():
        hb_ref[...] = _rms(x_ref[...], g_ref[...]).astype(BF16)
        acc_ref[...] = jnp.zeros(acc_ref.shape, F32)

    acc_ref[...] += _swiglu_chunk(hb_ref[...], w1_ref, w3_ref, w2_ref)

    @pl.when(f == pl.num_programs(1) - 1)
    def _():
        y = x_ref[...] + acc_ref[...]
        o_ref[...] = _rms(y, gf_ref[...]) if final_norm else y


def _ffn(x2, gain, w1, w3, w2, gain_final, final_norm, tm, tf):
    n, D = x2.shape
    F = w1.shape[1]
    return pl.pallas_call(
        functools.partial(_ffn_kernel, final_norm=final_norm),
        grid=(n // tm, F // tf),
        in_specs=[
            pl.BlockSpec((tm, D), lambda i, f: (i, 0)),
            pl.BlockSpec((1, D), lambda i, f: (0, 0)),
            pl.BlockSpec((D, tf), lambda i, f: (0, f)),
            pl.BlockSpec((D, tf), lambda i, f: (0, f)),
            pl.BlockSpec((tf, D), lambda i, f: (f, 0)),
            pl.BlockSpec((1, D), lambda i, f: (0, 0)),
        ],
        out_specs=pl.BlockSpec((tm, D), lambda i, f: (i, 0)),
        out_shape=jax.ShapeDtypeStruct((n, D), F32),
        scratch_shapes=[pltpu.VMEM((tm, D), BF16), pltpu.VMEM((tm, D), F32)],
        compiler_params=_cparams(("parallel", "arbitrary")),
        name="ffn",
    )(x2, gain.reshape(1, D), w1, w3, w2, gain_final.reshape(1, D))


def _moe_kernel(x_ref, g_ref, wr_ref, w1_ref, w3_ref, w2_ref, gf_ref, o_ref, hb_ref, gate_ref, acc_ref, *,
                final_norm):
    e = pl.program_id(1)
    f = pl.program_id(2)

    @pl.when((e == 0) & (f == 0))
    def _():
        hb = _rms(x_ref[...], g_ref[...]).astype(BF16)
        hb_ref[...] = hb
        acc_ref[...] = jnp.zeros(acc_ref.shape, F32)
        logits = _dot(hb, wr_ref[...])
        lane = lax.broadcasted_iota(jnp.int32, logits.shape, 1)
        lanef = lane.astype(F32)
        logits = jnp.where(lane < N_EXPERTS, logits, NEG)
        v1 = jnp.max(logits, axis=-1, keepdims=True)
        i1 = jnp.min(jnp.where(logits == v1, lanef, float(LANES)), axis=-1, keepdims=True)
        rest = jnp.where(lanef == i1, REMOVED, logits)
        v2 = jnp.max(rest, axis=-1, keepdims=True)
        i2 = jnp.min(jnp.where(rest == v2, lanef, float(LANES)), axis=-1, keepdims=True)
        e2 = jnp.exp(v2 - v1)
        w_1 = 1.0 / (1.0 + e2)
        w_2 = e2 / (1.0 + e2)
        gate_ref[...] = jnp.where(lanef == i1, w_1, 0.0) + jnp.where(lanef == i2, w_2, 0.0)

    lane = lax.broadcasted_iota(jnp.int32, gate_ref.shape, 1)
    gate_e = jnp.sum(jnp.where(lane == e, gate_ref[...], 0.0), axis=-1, keepdims=True)
    acc_ref[...] += gate_e * _swiglu_chunk(hb_ref[...], w1_ref, w3_ref, w2_ref)

    @pl.when((e == pl.num_programs(1) - 1) & (f == pl.num_programs(2) - 1))
    def _():
        y = x_ref[...] + acc_ref[...]
        o_ref[...] = _rms(y, gf_ref[...]) if final_norm else y


def _moe(x2, gain, w_r, w1, w3, w2, gain_final, final_norm, tm, tf):
    n, D = x2.shape
    E, _, F = w1.shape
    return pl.pallas_call(
        functools.partial(_moe_kernel, final_norm=final_norm),
        grid=(n // tm, E, F // tf),
        in_specs=[
            pl.BlockSpec((tm, D), lambda i, e, f: (i, 0)),
            pl.BlockSpec((1, D), lambda i, e, f: (0, 0)),
            pl.BlockSpec((D, LANES), lambda i, e, f: (0, 0)),
            pl.BlockSpec((None, D, tf), lambda i, e, f: (e, 0, f)),
            pl.BlockSpec((None, D, tf), lambda i, e, f: (e, 0, f)),
            pl.BlockSpec((None, tf, D), lambda i, e, f: (e, f, 0)),
            pl.BlockSpec((1, D), lambda i, e, f: (0, 0)),
        ],
        out_specs=pl.BlockSpec((tm, D), lambda i, e, f: (i, 0)),
        out_shape=jax.ShapeDtypeStruct((n, D), F32),
        scratch_shapes=[pltpu.VMEM((tm, D), BF16), pltpu.VMEM((tm, LANES), F32), pltpu.VMEM((tm, D), F32)],
        compiler_params=_cparams(("parallel", "arbitrary", "arbitrary")),
        name="moe",
    )(x2, gain.reshape(1, D), w_r, w1, w3, w2, gain_final.reshape(1, D))


SROWS = 16


def _page_specs(block, kind_block, layer, n_fixed):
    tail = (0,) * (len(block) - 3)

    def spec(j):
        def index_map(b, c, pt, *_):
            return (layer, pt[b, c * PAGES_PER_STEP + j], kind_block) + tail
        return pl.BlockSpec(block, index_map)

    del n_fixed
    return [spec(j) for j in range(PAGES_PER_STEP)]


def _stack_rows(top, bot):
    pad = jnp.zeros((SROWS - top.shape[0] - bot.shape[0], LANES), F32)
    return jnp.concatenate([top, bot, pad], axis=0)


def _adec_kernel(pt_ref, q_ref, ks_ref, vs_ref, blast_ref, b0_ref, lam_ref, sub_ref, *rest, lam_init):
    pages = rest[:PAGES_PER_STEP]
    o_ref, qs_ref, m_ref, l_ref, acc_ref = rest[PAGES_PER_STEP:]
    c = pl.program_id(1)
    last = pl.num_programs(1) - 1
    lane2 = lax.broadcasted_iota(jnp.int32, (2, LANES), 1)

    @pl.when(c == 0)
    def _():
        for g in range(KV_A):
            qq = q_ref[2 * g:2 * g + 2]
            qg = _stack_rows(jnp.where(lane2 < HD, qq, 0.0), jnp.where(lane2 >= HD, qq, 0.0))
            qs_ref[g] = qg
            k_self = ks_ref[g:g + 1].astype(BF16).astype(F32)
            s_self = jnp.sum(qg.astype(BF16).astype(F32) * k_self, axis=-1, keepdims=True)
            m_ref[g] = s_self + b0_ref[g]
            l_ref[g] = jnp.ones((SROWS, LANES), F32)
            acc_ref[g] = jnp.broadcast_to(vs_ref[g:g + 1].astype(BF16).astype(F32), (SROWS, LANES))

    for g in range(KV_A):
        kg = jnp.concatenate([p[pl.ds(g, PAGE, stride=2 * KV_A), :] for p in pages], axis=0).astype(BF16)
        vg = jnp.concatenate([p[pl.ds(KV_A + g, PAGE, stride=2 * KV_A), :] for p in pages], axis=0).astype(BF16)
        s = _dot_nt(qs_ref[g].astype(BF16), kg)
        s = s + jnp.where(c == last, blast_ref[g], 0.0)
        _softmax_step(s, vg, m_ref, l_ref, acc_ref, g)

    @pl.when(c == last)
    def _():
        for g in range(KV_A):
            o = acc_ref[g] / jnp.maximum(l_ref[g], 1e-30)
            o_ref[2 * g:2 * g + 2, :] = _diff_finish(o[0:2], o[2:4], lam_ref[...], sub_ref[...], lam_init)


def _head_rows_a(vals):
    out = []
    for g in range(KV_A):
        r = [vals[2 * g], vals[2 * g + 1], vals[2 * g], vals[2 * g + 1]]
        out.append(jnp.stack(r + [jnp.zeros_like(vals[0])] * (SROWS - 4)))
    return jnp.stack(out)


def _head_rows_b(vals):
    out = []
    for g in range(KV_B):
        r = [vals[_group_head(g, rb)] for rb in range(R_B)]
        out.append(jnp.stack(r + [jnp.zeros_like(vals[0])] * (SROWS - R_B)))
    return jnp.stack(out)


def _adec(page_table, cache_view, layer, q, k_self, v_self, bias_a, lam_p, subln, lam_init):
    DB, NP = page_table.shape
    far = bias_a[N_BUCKETS - 1][:, None]
    kw = PAGES_PER_STEP * PAGE
    near = _bias_of_dist(bias_a, PAGE - np.arange(PAGE)) - far
    blast = _head_rows_a(jnp.pad(near, ((0, 0), (kw - PAGE, 0))))
    b0 = _head_rows_a((bias_a[0][:, None] - far) * jnp.ones((1, LANES), F32))
    per_b = lambda *blk: pl.BlockSpec((None,) + blk, lambda b, c, pt: (b,) + (0,) * len(blk))
    const = lambda *blk: pl.BlockSpec(blk, lambda b, c, pt: (0,) * len(blk))
    return pl.pallas_call(
        functools.partial(_adec_kernel, lam_init=lam_init),
        grid_spec=pltpu.PrefetchScalarGridSpec(
            num_scalar_prefetch=1,
            grid=(DB, NP // PAGES_PER_STEP),
            in_specs=[per_b(H_A, LANES), per_b(KV_A, LANES), per_b(KV_A, LANES),
                      const(KV_A, SROWS, kw), const(KV_A, SROWS, LANES), const(4, HD), const(1, 2 * HD)]
            + _page_specs((None, None, PAGE * 2 * KV_A, LANES), 0, layer, 0),
            out_specs=per_b(H_A, LANES),
            scratch_shapes=[pltpu.VMEM((KV_A, SROWS, LANES), F32)] * 4,
        ),
        out_shape=jax.ShapeDtypeStruct((DB, H_A, LANES), F32),
        compiler_params=_cparams(("parallel", "arbitrary")),
        name="adec",
    )(page_table, q, k_self, v_self, blast, b0, lam_p, subln.reshape(1, 2 * HD),
      *([cache_view] * PAGES_PER_STEP))


def _stack_pairs(q4):
    lane = lax.broadcasted_iota(jnp.int32, q4.shape, 1)
    return _stack_rows(jnp.where(lane < HD, q4, 0.0), jnp.where(lane >= HD, q4, 0.0))


def _top_indices(score, n):
    lanef = lax.broadcasted_iota(jnp.int32, score.shape, 1).astype(F32)
    slot = lax.broadcasted_iota(jnp.int32, (score.shape[0], LANES), 1)
    width = float(score.shape[1])

    def body(it, carry):
        sc, out = carry
        mx = jnp.max(sc, axis=1, keepdims=True)
        idx = jnp.min(jnp.where(sc == mx, lanef, width), axis=1, keepdims=True)
        return jnp.where(lanef == idx, REMOVED, sc), jnp.where(slot == it, idx.astype(jnp.int32), out)

    return lax.fori_loop(0, n, body, (score, jnp.zeros((score.shape[0], LANES), jnp.int32)))[1]


def _ncmp_kernel(pt_ref, q_ref, new_ref, bc_ref, pe_ref, w1_ref, w2_ref, c2s_ref, *rest, past):
    pages = rest[:PAGES_PER_STEP]
    oc_ref, idx_ref, xk_ref, xv_ref, p_ref = rest[PAGES_PER_STEP:]
    c = pl.program_id(1)
    stage = (xk_ref, xv_ref)
    for j, page in enumerate(pages):
        r0 = pl.multiple_of((c * PAGES_PER_STEP + j) * PAGE, PAGE)
        for kind in range(2):
            stage[kind][pl.ds(r0, PAGE), :] = page[kind].reshape(KV_B * HD, PAGE).T

    @pl.when(c == pl.num_programs(1) - 1)
    def _():
        nblk = past // CMP_STRIDE
        trow = lax.broadcasted_iota(jnp.int32, (2 * CMP_STRIDE, LANES), 0)
        res = []
        for kind in range(2):
            stage[kind][past:past + 2 * CMP_STRIDE, :] = jnp.where(trow == 0, new_ref[kind], 0.0)
            res.append(_compress_core(stage[kind], nblk, pe_ref.at[kind], w1_ref.at[kind], w2_ref.at[kind],
                                      p_ref).astype(BF16))
        ns_pad = c2s_ref.shape[1]
        for g in range(KV_B):
            qg = _stack_pairs(q_ref[4 * g:4 * g + 4]).astype(BF16)
            s = _dot_nt(qg, res[0][:, g * LANES:(g + 1) * LANES]) + bc_ref[g]
            e = jnp.exp(s - jnp.max(s, axis=-1, keepdims=True))
            pc = (e / jnp.maximum(jnp.sum(e, axis=-1, keepdims=True), 1e-30)).astype(BF16)
            oc_ref[g] = _dot(pc, res[1][:, g * LANES:(g + 1) * LANES])
            imp = jnp.sum(_dot(pc, c2s_ref[...])[0:R_B], axis=0, keepdims=True)
            blk = lax.broadcasted_iota(jnp.int32, (R_B, ns_pad), 1)
            cur = past // SEL_BLOCK
            forced = (blk == 0) | (blk == cur) | (blk == cur - 1)
            score = jnp.where(blk <= cur, jnp.where(forced, FORCE, jnp.broadcast_to(imp, (R_B, ns_pad))), NEG)
            idx_ref[g] = _top_indices(score, SEL_TOPN)


def _ncmp(page_table, nsa_view, layer, qb, new_cmp, bias_b, pe_flat, w1bd, w2bd):
    DB, NP = page_table.shape
    past = NP * PAGE
    nblk = past // CMP_STRIDE
    ns = past // SEL_BLOCK + 1
    ns_pad = -(-ns // LANES) * LANES
    i = np.arange(nblk)
    dist = past - (i * CMP_STRIDE + CMP_LEN - 1)
    bc = _head_rows_b(jnp.where(jnp.asarray(dist >= 0), _bias_of_dist(bias_b, dist), NEG))
    ci = i[:, None] * CMP_STRIDE
    lo = np.arange(ns_pad)[None, :] * SEL_BLOCK
    c2s = jnp.asarray(((ci < lo + SEL_BLOCK) & (ci + CMP_LEN > lo) & (lo < ns * SEL_BLOCK)).astype(np.float32))
    per_b = lambda *blk: pl.BlockSpec((None,) + blk, lambda b, c, pt: (b,) + (0,) * len(blk))
    const = lambda *blk: pl.BlockSpec(blk, lambda b, c, pt: (0,) * len(blk), pipeline_mode=pl.Buffered(1))
    return pl.pallas_call(
        functools.partial(_ncmp_kernel, past=past),
        grid_spec=pltpu.PrefetchScalarGridSpec(
            num_scalar_prefetch=1,
            grid=(DB, NP // PAGES_PER_STEP),
            in_specs=[per_b(H_B // 2, LANES), per_b(2, 1, LANES), const(KV_B, SROWS, nblk),
                      const(2, 1, CMP_K), const(2, CMP_K, 2 * CMP_HID), const(2, 2 * CMP_HID, 2 * LANES),
                      const(nblk, ns_pad)]
            + _page_specs((None, None, 2, KV_B, HD, PAGE), 0, layer, 0),
            out_specs=[per_b(KV_B, SROWS, LANES), per_b(KV_B, R_B, LANES)],
            scratch_shapes=[pltpu.VMEM((past + 2 * CMP_STRIDE, LANES), F32),
                            pltpu.VMEM((past + 2 * CMP_STRIDE, LANES), F32),
                            pltpu.VMEM((nblk, CMP_K), BF16)],
        ),
        out_shape=[jax.ShapeDtypeStruct((DB, KV_B, SROWS, LANES), F32),
                   jax.ShapeDtypeStruct((DB, KV_B, R_B, LANES), jnp.int32)],
        compiler_params=_cparams(("parallel", "arbitrary")),
        name="ncmp",
    )(page_table, qb, new_cmp, bc, pe_flat, w1bd, w2bd, c2s.astype(BF16), *([nsa_view] * PAGES_PER_STEP))


def _dup_t(x):
    return jnp.concatenate([x, x], axis=0).astype(BF16)


def _nsel_kernel(pt_ref, idx_ref, q_ref, gb_ref, oc_ref, new_ref, newcol_ref, blast_ref, b0_ref,
                 bw_ref, win_ref, ex_ref, *rest, n_pages):
    sel_pages = rest[:KV_B * SEL_TOPN]
    o_ref, wout_ref = rest[KV_B * SEL_TOPN:]
    b = pl.program_id(0)
    lane = lax.broadcasted_iota(jnp.int32, (SROWS, LANES), 1)
    lane4 = lax.broadcasted_iota(jnp.int32, (4, LANES), 1)
    g8 = jnp.broadcast_to(gb_ref[...], (8, LANES))
    g_hi = g8.astype(BF16)
    g_lo = (g8 - g_hi.astype(F32)).astype(BF16)
    gexp = [_dot(g_hi, ex_ref[br]) + _dot(g_lo, ex_ref[br]) for br in range(3)]

    def unstack(o):
        return jnp.where(lane4 < HD, o[0:4], o[4:8])

    def rounded(x):
        return x.astype(BF16).astype(F32)

    for g in range(KV_B):
        qg = _stack_pairs(q_ref[4 * g:4 * g + 4])
        qb = qg.astype(BF16)
        qr = rounded(qg)

        def self_score(k):
            return jnp.sum(qr * rounded(new_ref[k:k + 1]), axis=-1, keepdims=True) + b0_ref[g][:, 0:1]

        s_self = self_score(g)
        scores = []
        for t in range(SEL_TOPN):
            blk = idx_ref[b, g * SEL_TOPN + t]
            page = sel_pages[g * SEL_TOPN + t]
            s = _dot(qb, _dup_t(page[0, g]))
            s = s + jnp.where(blk // 2 == n_pages - 1, blast_ref[g], 0.0)
            ok = (blk < 2 * n_pages) & ((lane // SEL_BLOCK) == blk % 2)
            scores.append(jnp.where(ok, s, NEG))
        m = s_self
        for s in scores:
            m = jnp.maximum(m, jnp.max(s, axis=-1, keepdims=True))
        e_self = jnp.exp(s_self - m)
        l = e_self
        acc = rounded(e_self) * rounded(new_ref[4 + g:5 + g])
        for t, s in enumerate(scores):
            e = jnp.exp(s - m)
            l = l + jnp.sum(e, axis=-1, keepdims=True)
            acc = acc + _dot_nt(e.astype(BF16), _dup_t(sel_pages[g * SEL_TOPN + t][1, g]))
        o_s = unstack(acc / jnp.maximum(l, 1e-30))

        s_self = self_score(2 + g)
        s = _dot(qb, _dup_t(win_ref[0, g])) + bw_ref[g]
        m = jnp.maximum(s_self, jnp.max(s, axis=-1, keepdims=True))
        e = jnp.exp(s - m)
        e_self = jnp.exp(s_self - m)
        l = e_self + jnp.sum(e, axis=-1, keepdims=True)
        acc = rounded(e_self) * rounded(new_ref[6 + g:7 + g]) + _dot_nt(e.astype(BF16), _dup_t(win_ref[1, g]))
        o_w = unstack(acc / jnp.maximum(l, 1e-30))
        o_c = unstack(oc_ref[g])

        for p in range(4):
            c0 = (4 * g + p) * LANES
            o_ref[:, c0:c0 + LANES] = (gexp[0][0:1, c0:c0 + LANES] * o_c[p:p + 1]
                                       + gexp[1][0:1, c0:c0 + LANES] * o_s[p:p + 1]
                                       + gexp[2][0:1, c0:c0 + LANES] * o_w[p:p + 1])

    n_win = win_ref.shape[3]
    wl = lax.broadcasted_iota(jnp.int32, (HD, n_win), 1)
    for kv in range(2):
        for g in range(KV_B):
            r0 = (kv * KV_B + g) * HD
            shifted = pltpu.roll(win_ref[kv, g], n_win - 1, axis=1)
            wout_ref[kv, g] = jnp.where(wl == n_win - 1, newcol_ref[r0:r0 + HD, :], shifted)


def _nsel(page_table, idx, nsa_view, win_view, layer, qb, gates, oc, new_rows, new_col, bias_b):
    DB, NP = page_table.shape
    n_win = win_view.shape[-1]
    far = bias_b[N_BUCKETS - 1][:, None]
    blast = _head_rows_b(_bias_of_dist(bias_b, PAGE - np.arange(PAGE)) - far)
    b0 = _head_rows_b((bias_b[0][:, None] - far) * jnp.ones((1, LANES), F32))
    bw = _head_rows_b(_bias_of_dist(bias_b, n_win - np.arange(n_win)) - far)
    k = np.arange(LANES)[:, None]
    col = np.arange(H_B * HD)[None, :]
    ex = jnp.asarray(np.stack([(k == (col // HD) * 3 + br) for br in range(3)]).astype(np.float32)).astype(BF16)
    per_b = lambda *blk: pl.BlockSpec((None,) + blk, lambda b, pt, ix: (b,) + (0,) * len(blk))
    const = lambda *blk: pl.BlockSpec(blk, lambda b, pt, ix: (0,) * len(blk))

    def sel_spec(g, t):
        def index_map(b, pt, ix):
            return (layer, pt[b, jnp.clip(ix[b, g * SEL_TOPN + t] // 2, 0, NP - 1)], 1, 0, 0, 0)
        return pl.BlockSpec((None, None, 2, KV_B, HD, PAGE), index_map)

    return pl.pallas_call(
        functools.partial(_nsel_kernel, n_pages=NP),
        grid_spec=pltpu.PrefetchScalarGridSpec(
            num_scalar_prefetch=2,
            grid=(DB,),
            in_specs=[per_b(H_B // 2, LANES), per_b(1, LANES), per_b(KV_B, SROWS, LANES), per_b(8, LANES),
                      per_b(4 * HD, 1), const(KV_B, SROWS, LANES), const(KV_B, SROWS, LANES),
                      const(KV_B, SROWS, n_win),
                      pl.BlockSpec((None, None, 2, KV_B, HD, n_win), lambda b, pt, ix: (layer, b, 0, 0, 0, 0)),
                      const(3, LANES, H_B * HD)]
            + [sel_spec(g, t) for g in range(KV_B) for t in range(SEL_TOPN)],
            out_specs=[per_b(1, H_B * HD), per_b(2, KV_B, HD, n_win)],
        ),
        out_shape=[jax.ShapeDtypeStruct((DB, 1, H_B * HD), F32),
                   jax.ShapeDtypeStruct((DB, 2, KV_B, HD, n_win), F32)],
        compiler_params=_cparams(("parallel",)),
        name="nsel",
    )(page_table, idx, qb, gates, oc, new_rows, new_col, blast, b0, bw, win_view, ex,
      *([nsa_view] * (KV_B * SEL_TOPN)))


def _row_tile(n, cap):
    return n if n <= cap else cap


def _ff_tile(F):
    return F // 2 if (F > 1408 and (F // 2) % LANES == 0) else F


def kernel(x_prompt, x_sample, cache_attn_kv, cache_nsa_kv, state_win_kv, page_table, rel_bias, norm_mix, w_in, w_a, w_b, w_o, diff_lambda, diff_subln, cmp_pe, cmp_w1, cmp_w2, norm_ffn, ffn_w1, ffn_w3, ffn_w2, moe_router, moe_w1, moe_w3, moe_w2, norm_final):
    B, T, D = x_prompt.shape
    DB, dec_t, _ = x_sample.shape
    assert dec_t == 1
    depth = w_in.shape[0]
    n_pool = cache_attn_kv.shape[1]
    n_win = state_win_kv.shape[2]
    bias_a, bias_b = rel_bias[:, :H_A], rel_bias[:, H_A:]
    attn_view = cache_attn_kv.reshape(depth, n_pool, PAGE * 2 * KV_A, LANES)
    nsa_view = jnp.transpose(cache_nsa_kv, (0, 1, 3, 4, 5, 2))
    win_view = jnp.transpose(state_win_kv, (0, 1, 3, 4, 5, 2))

    xp = x_prompt.reshape(B * T, D)
    xs = x_sample.reshape(DB, D)
    tm = _row_tile(T, 512)
    att_t = _row_tile(T, ATT_T)
    outs = [[] for _ in range(6)]
    for l in range(depth):
        lam_init = 0.8 - 0.6 * math.exp(-0.3 * l)
        last = l == depth - 1
        w_row, w_t, w_gm = _prep_in_weights(w_in[l])
        wa, wb, wo = w_a[l].astype(BF16), w_b[l].astype(BF16), w_o[l].astype(BF16)
        pe_flat, w1bd, w2bd = _prep_cmp_weights(cmp_pe[l], cmp_w1[l], cmp_w2[l])

        qa4, kva, va, qb4, cmp_rows, vdup, gb, kat, nsat, wint, kdupt = _proj(
            xp.reshape(B, T, D), norm_mix[l], w_row, w_t, tm)
        a = _dattn(qa4, kat, va, bias_a, diff_lambda[l], diff_subln[l], lam_init, att_t)
        kcv = _compress(cmp_rows, B, T, pe_flat, w1bd, w2bd)
        ob = _nsa(qb4, gb, kcv, kdupt, vdup, bias_b, NSA_TQ, NSA_TK)
        xp = _merge(xp, norm_mix[l], a, ob, w_gm, wa, wb, wo, tm)
        outs[0].append(kva.reshape(B, T, 2, KV_A, 2 * HD))
        outs[2].append(jnp.transpose(nsat.reshape(B, 4, KV_B, HD, T), (0, 4, 1, 2, 3)))
        wlen = min(WINDOW, T)
        outs[4].append(jnp.transpose(wint[:, :, T - wlen:].reshape(B, 2, KV_B, HD, wlen), (0, 4, 1, 2, 3)))

        qa_s, kva_s, _, qb_s, cmp_s, vdup_s, gb_s, _, nsat_s, wint_s, kdupt_s = _proj(
            xs.reshape(1, DB, D), norm_mix[l], w_row, w_t, DB)
        kva_s = kva_s.reshape(DB, 2, KV_A, 2 * HD)
        q_a = jnp.transpose(qa_s[0], (1, 0, 2)).astype(F32)
        q_b = jnp.transpose(qb_s[0], (1, 0, 2)).astype(F32)
        a_s = _adec(page_table, attn_view, l, q_a, kva_s[:, 0], kva_s[:, 1], bias_a, diff_lambda[l],
                    diff_subln[l], lam_init)
        oc, idx = _ncmp(page_table, nsa_view, l, q_b, cmp_s.reshape(DB, 2, 1, LANES), bias_b, pe_flat, w1bd, w2bd)
        new_rows = jnp.concatenate([jnp.transpose(kdupt_s[0], (2, 0, 1)), jnp.transpose(vdup_s[0], (1, 0, 2))],
                                   axis=1).astype(F32)
        new_col = jnp.transpose(wint_s[0]).reshape(DB, 4 * HD, 1)
        ob_s, wout = _nsel(page_table, idx[:, :, 0, :SEL_TOPN].reshape(DB, KV_B * SEL_TOPN), nsa_view, win_view, l, q_b,
                           gb_s.reshape(DB, 1, LANES), oc, new_rows, new_col, bias_b)
        xs = _merge(xs, norm_mix[l], a_s.reshape(DB, H_A * 2 * HD).astype(BF16),
                    ob_s.reshape(DB, H_B * HD).astype(BF16), w_gm, wa, wb, wo, DB)
        outs[1].append(kva_s.reshape(DB, 1, 2, KV_A, 2 * HD))
        outs[3].append(jnp.transpose(nsat_s[0]).reshape(DB, 1, 4, KV_B, HD))
        outs[5].append(jnp.transpose(wout, (0, 4, 1, 2, 3)))

        e = l // 2
        if l % 2 == 0:
            w1, w3, w2 = ffn_w1[e].astype(BF16), ffn_w3[e].astype(BF16), ffn_w2[e].astype(BF16)
            tf = _ff_tile(w1.shape[1])
            xp = _ffn(xp, norm_ffn[l], w1, w3, w2, norm_final, last, tm, tf)
            xs = _ffn(xs, norm_ffn[l], w1, w3, w2, norm_final, last, DB, tf)
        else:
            w1, w3, w2 = moe_w1[e].astype(BF16), moe_w3[e].astype(BF16), moe_w2[e].astype(BF16)
            w_r = jnp.pad(moe_router[e], ((0, 0), (0, LANES - N_EXPERTS))).astype(BF16)
            tf = _ff_tile(w1.shape[2])
            xp = _moe(xp, norm_ffn[l], w_r, w1, w3, w2, norm_final, last, _row_tile(B * T, 1024), tf)
            xs = _moe(xs, norm_ffn[l], w_r, w1, w3, w2, norm_final, last, DB, tf)

    return (xp.reshape(B, T, D), xs.reshape(DB, 1, D)) + tuple(jnp.stack(o) for o in outs)
```

```python
import functools
import math

import numpy as np
import jax
import jax.numpy as jnp
from jax import lax
from jax.experimental import pallas as pl
from jax.experimental.pallas import tpu as pltpu

F32 = jnp.float32
BF16 = jnp.bfloat16

HD = 64
H_A = 8
KV_A = 4
H_B = 16
KV_B = 2
R_B = H_B // KV_B
CMP_LEN = 32
CMP_STRIDE = 16
CMP_HID = 128
SEL_BLOCK = 64
SEL_TOPN = 16
WINDOW = 512
N_BUCKETS = 32
MAX_DIST = 128
N_EXPERTS = 8
PAGE = 128
EPS = 1e-6
NEG = -1e30
FORCE = 1e9
REMOVED = -3e38

LANES = 128
VMEM_LIMIT = 56 * 1024 * 1024

ATT_T = 512
NSA_TQ = 128
NSA_TK = 512
BAND_W = 32
BAND_LEAD = 16
PAGES_PER_STEP = 8

N_ROW = 3968
N_T = 1792


def _cparams(sem):
    return pltpu.CompilerParams(dimension_semantics=sem, vmem_limit_bytes=VMEM_LIMIT)


def _resident(block, index_map):
    return pl.BlockSpec(block, index_map, pipeline_mode=pl.Buffered(1))


def _rms(x, g):
    xf = x.astype(F32)
    return xf * lax.rsqrt(jnp.mean(xf * xf, axis=-1, keepdims=True) + EPS) * g


def _dot(a, b):
    return jnp.dot(a, b, preferred_element_type=F32)


def _dot_nt(a, b):
    return lax.dot_general(a, b, (((1,), (1,)), ((), ())), preferred_element_type=F32)


def _bucket_np(d):
    d = np.maximum(np.asarray(d, np.int64), 0)
    exact = N_BUCKETS // 2
    df = np.maximum(d, 1).astype(np.float32)
    far = exact + (np.log(df / np.float32(exact)) / np.float32(math.log(MAX_DIST / exact))
                   * np.float32(N_BUCKETS - exact)).astype(np.int64)
    return np.where(d < exact, d, np.minimum(far, N_BUCKETS - 1))


FAR_DIST = 128
assert np.all(_bucket_np(np.arange(FAR_DIST, 1 << 16)) == N_BUCKETS - 1)


def _bias_of_dist(table, dist):
    idx = jnp.asarray(_bucket_np(dist).astype(np.int32))
    return jnp.moveaxis(jnp.take(table, idx, axis=0), -1, 0)


def _proj_kernel(x_ref, g_ref, wr_ref, wt_ref, qa_ref, kva_ref, va_ref, qb_ref, cmp_ref, vdup_ref, gb_ref,
                 kat_ref, nsat_ref, wint_ref, kdupt_ref):
    tm = x_ref.shape[0]
    hb = _rms(x_ref[...], g_ref[...]).astype(BF16)

    def cols(a, b):
        return _dot(hb, wr_ref[:, a:b])

    z = cols(0, 1024)
    for h in range(H_A):
        qa_ref[h] = z[:, h * LANES:(h + 1) * LANES].astype(BF16)
    z = cols(1024, 2048)
    for c in range(2 * KV_A):
        kva_ref[pl.ds(c, tm, stride=2 * KV_A), :] = z[:, c * LANES:(c + 1) * LANES]
    va_ref[...] = z[:, 512:1024].astype(BF16)
    z = cols(2048, 3072)
    for p in range(H_B // 2):
        qb_ref[p] = z[:, p * LANES:(p + 1) * LANES].astype(BF16)
    cmp_ref[...] = cols(3072, 3328)
    z = cols(3328, 3840)
    for k in range(4):
        vdup_ref[k] = z[:, k * LANES:(k + 1) * LANES].astype(BF16)
    gb_ref[...] = jax.nn.sigmoid(cols(3840, N_ROW))

    def rows(a, b):
        return _dot_nt(wt_ref[a:b, :], hb)

    z = rows(0, 512)
    for g in range(KV_A):
        kat_ref[g] = z[g * LANES:(g + 1) * LANES].astype(BF16)
    nsat_ref[...] = rows(512, 1024)
    wint_ref[...] = rows(1024, 1280)
    z = rows(1280, N_T)
    for k in range(4):
        kdupt_ref[k] = z[k * LANES:(k + 1) * LANES].astype(BF16)


def _proj(x, gain, w_row, w_t, tm):
    Bp, T, D = x.shape
    nt = T // tm
    n = Bp * T
    row = lambda b, i: (b * nt + i, 0)
    out_shape = (
        jax.ShapeDtypeStruct((Bp, H_A, T, LANES), BF16),
        jax.ShapeDtypeStruct((n * 2 * KV_A, LANES), F32),
        jax.ShapeDtypeStruct((n, KV_A * 2 * HD), BF16),
        jax.ShapeDtypeStruct((Bp, H_B // 2, T, LANES), BF16),
        jax.ShapeDtypeStruct((n, 4 * HD), F32),
        jax.ShapeDtypeStruct((Bp, 4, T, LANES), BF16),
        jax.ShapeDtypeStruct((n, LANES), F32),
        jax.ShapeDtypeStruct((Bp, KV_A, LANES, T), BF16),
        jax.ShapeDtypeStruct((Bp, 512, T), F32),
        jax.ShapeDtypeStruct((Bp, 256, T), F32),
        jax.ShapeDtypeStruct((Bp, 4, LANES, T), BF16),
    )
    out_specs = (
        pl.BlockSpec((None, H_A, tm, LANES), lambda b, i: (b, 0, i, 0)),
        pl.BlockSpec((tm * 2 * KV_A, LANES), row),
        pl.BlockSpec((tm, 512), row),
        pl.BlockSpec((None, H_B // 2, tm, LANES), lambda b, i: (b, 0, i, 0)),
        pl.BlockSpec((tm, 256), row),
        pl.BlockSpec((None, 4, tm, LANES), lambda b, i: (b, 0, i, 0)),
        pl.BlockSpec((tm, LANES), row),
        pl.BlockSpec((None, KV_A, LANES, tm), lambda b, i: (b, 0, 0, i)),
        pl.BlockSpec((None, 512, tm), lambda b, i: (b, 0, i)),
        pl.BlockSpec((None, 256, tm), lambda b, i: (b, 0, i)),
        pl.BlockSpec((None, 4, LANES, tm), lambda b, i: (b, 0, 0, i)),
    )
    return pl.pallas_call(
        _proj_kernel,
        grid=(Bp, nt),
        in_specs=[
            pl.BlockSpec((None, tm, D), lambda b, i: (b, i, 0)),
            pl.BlockSpec((1, D), lambda b, i: (0, 0)),
            _resident((D, N_ROW), lambda b, i: (0, 0)),
            _resident((N_T, D), lambda b, i: (0, 0)),
        ],
        out_specs=out_specs,
        out_shape=out_shape,
        compiler_params=_cparams(("parallel", "parallel")),
        name="proj",
    )(x, gain.reshape(1, D), w_row, w_t)


def _prep_in_weights(w_in_l):
    wt = w_in_l.T
    D = wt.shape[1]
    scale = HD ** -0.5
    qa, ka, va = wt[0:1024] * scale, wt[1024:1536], wt[1536:2048]
    qb = wt[2048:3072] * scale
    kvb = wt[3072:3840].reshape(6, KV_B, HD, D)
    gb = wt[3840:3888]
    gm = wt[3888:]
    dup = lambda w: jnp.concatenate([w, w], axis=0)
    vdup = jnp.concatenate([dup(kvb[k, g]) for k in (3, 5) for g in range(KV_B)], axis=0)
    kdup = jnp.concatenate([dup(kvb[k, g]) for k in (2, 4) for g in range(KV_B)], axis=0)
    w_row_t = jnp.concatenate(
        [qa, ka, va, qb, kvb[0:2].reshape(256, D), vdup, gb, jnp.zeros((LANES - gb.shape[0], D), F32)], axis=0)
    w_t = jnp.concatenate([ka, kvb[0:4].reshape(512, D), kvb[4:6].reshape(256, D), kdup], axis=0)
    return w_row_t.T.astype(BF16), w_t.astype(BF16), gm.T.astype(BF16)


def _softmax_step(s, v, m_ref, l_ref, acc_ref, idx):
    m_old = m_ref[idx]
    m_new = jnp.maximum(m_old, jnp.max(s, axis=-1, keepdims=True))
    p = jnp.exp(s - pltpu.repeat(m_new, s.shape[1] // LANES, axis=1))
    alpha = jnp.exp(m_old - m_new)
    l_ref[idx] = alpha * l_ref[idx] + jnp.sum(p, axis=-1, keepdims=True)
    acc_ref[idx] = alpha * acc_ref[idx] + _dot(p.astype(BF16), v)
    m_ref[idx] = m_new


def _toeplitz(u, n):
    H = u.shape[0]
    wp = jnp.pad(u[:, ::-1], ((0, 0), (0, 1)))
    r = jnp.broadcast_to(wp[:, None, :], (H, n, 2 * n)).reshape(H, 2 * n * n)[:, :n * (2 * n - 1)]
    return r.reshape(H, n, 2 * n - 1)[:, :, n - 1:]


def _near_tiles(table, t, masked):
    far = table[N_BUCKETS - 1][:, None]
    k = np.arange(2 * t - 1)
    d0 = jnp.where(jnp.asarray(k >= t - 1), _bias_of_dist(table, k - (t - 1)) - far, NEG if masked else 0.0)
    d1 = _bias_of_dist(table, k + 1) - far
    return jnp.stack([_toeplitz(d0, t), _toeplitz(d1, t)], axis=1)


def _diff_lambda(lp, lam_init):
    a = jnp.sum(lp[0:1] * lp[1:2], axis=-1, keepdims=True)
    b = jnp.sum(lp[2:3] * lp[3:4], axis=-1, keepdims=True)
    return jnp.exp(a) - jnp.exp(b) + lam_init


def _diff_finish(o0, o1, lp, subln, lam_init):
    o = o0 - _diff_lambda(lp, lam_init) * o1
    return _rms(o, subln) * (1.0 - lam_init)


def _dattn_kernel(q_ref, kt_ref, v_ref, bias_ref, lam_ref, sub_ref, o_ref, m_ref, l_ref, acc_ref, *, tq, lam_init):
    qi = pl.program_id(2)
    lane = lax.broadcasted_iota(jnp.int32, (tq, LANES), 1)
    m_ref[...] = jnp.full(m_ref.shape, NEG, F32)
    l_ref[...] = jnp.zeros(l_ref.shape, F32)
    acc_ref[...] = jnp.zeros(acc_ref.shape, F32)

    def step(start, width, kind):
        kt = kt_ref[:, pl.ds(start, width)]
        v = v_ref[pl.ds(start, width), :]
        for r in range(2):
            q = q_ref[r]
            zero = jnp.zeros_like(q)
            for c in range(2):
                s = _dot(jnp.where(lane < HD if c == 0 else lane >= HD, q, zero), kt)
                if kind is not None:
                    s = s + bias_ref[r, kind]
                _softmax_step(s, v, m_ref, l_ref, acc_ref, (c, slice(r * tq, (r + 1) * tq)))

    n_far = jnp.maximum(qi - 1, 0)

    def far_body(j, carry):
        step(pl.multiple_of(j * 2 * tq, 2 * tq), 2 * tq, None)
        return carry

    lax.fori_loop(0, n_far // 2, far_body, 0)

    @pl.when(n_far % 2 == 1)
    def _():
        step(pl.multiple_of((n_far - 1) * tq, tq), tq, None)

    @pl.when(qi >= 1)
    def _():
        step(pl.multiple_of((qi - 1) * tq, tq), tq, 1)

    step(pl.multiple_of(qi * tq, tq), tq, 0)

    o0 = acc_ref[0] / jnp.maximum(l_ref[0], 1e-30)
    o1 = acc_ref[1] / jnp.maximum(l_ref[1], 1e-30)
    y = _diff_finish(o0, o1, lam_ref[...], sub_ref[...], lam_init).astype(BF16)
    o_ref[:, 0:LANES] = y[:tq]
    o_ref[:, LANES:2 * LANES] = y[tq:]


def _dattn(qa4, kat, va, bias_a, lam_p, subln, lam_init, tq):
    B, _, T, _ = qa4.shape
    nq = T // tq
    tiles = _near_tiles(bias_a, tq, masked=True)
    return pl.pallas_call(
        functools.partial(_dattn_kernel, tq=tq, lam_init=lam_init),
        grid=(B, KV_A, nq),
        in_specs=[
            pl.BlockSpec((None, 2, tq, LANES), lambda b, g, i: (b, g, i, 0)),
            pl.BlockSpec((None, None, LANES, T), lambda b, g, i: (b, g, 0, 0)),
            pl.BlockSpec((T, LANES), lambda b, g, i: (b, g)),
            pl.BlockSpec((2, 2, tq, tq), lambda b, g, i: (g, 0, 0, 0)),
            pl.BlockSpec((4, HD), lambda b, g, i: (0, 0)),
            pl.BlockSpec((1, 2 * HD), lambda b, g, i: (0, 0)),
        ],
        out_specs=pl.BlockSpec((tq, 2 * LANES), lambda b, g, i: (b * nq + i, g)),
        out_shape=jax.ShapeDtypeStruct((B * T, H_A * 2 * HD), BF16),
        scratch_shapes=[pltpu.VMEM((2, 2 * tq, LANES), F32)] * 3,
        compiler_params=_cparams(("parallel", "parallel", "arbitrary")),
        name="dattn",
    )(qa4, kat, va, tiles, lam_p, subln.reshape(1, 2 * HD))


CMP_K = CMP_LEN * KV_B * HD


def _compress_core(xs_ref, nblk, pe_ref, w1_ref, w2_ref, p_ref):
    for j in range(CMP_LEN):
        xj = xs_ref[pl.ds(j, nblk, stride=CMP_STRIDE), :]
        p_ref[:, j * LANES:(j + 1) * LANES] = (xj + pe_ref[:, j * LANES:(j + 1) * LANES]).astype(BF16)
    hid = jax.nn.gelu(_dot(p_ref[...], w1_ref[...]))
    return _dot(hid.astype(BF16), w2_ref[...])


def _compress_kernel(x_ref, pe_ref, w1_ref, w2_ref, o_ref, xs_ref, p_ref, *, T):
    xs_ref[0:T, :] = x_ref[...]
    xs_ref[T:T + CMP_STRIDE, :] = jnp.zeros((CMP_STRIDE, LANES), F32)
    res = _compress_core(xs_ref, T // CMP_STRIDE, pe_ref, w1_ref, w2_ref, p_ref)
    o_ref[0] = res[:, 0:LANES].astype(BF16)
    o_ref[1] = res[:, LANES:2 * LANES].astype(BF16)


def _prep_cmp_weights(pe, w1, w2):
    pe_flat = jnp.broadcast_to(pe[:, :, None, :], (2, CMP_LEN, KV_B, HD)).reshape(2, 1, CMP_K)
    w1r = w1.reshape(2, CMP_LEN, HD, CMP_HID)
    w1bd = jnp.zeros((2, CMP_LEN, KV_B, HD, KV_B, CMP_HID), F32)
    w2bd = jnp.zeros((2, KV_B, CMP_HID, KV_B, 2, HD), F32)
    for g in range(KV_B):
        w1bd = w1bd.at[:, :, g, :, g, :].set(w1r)
        w2bd = w2bd.at[:, g, :, g, :, :].set(jnp.broadcast_to(w2[:, :, None, :], (2, CMP_HID, 2, HD)))
    return (pe_flat, w1bd.reshape(2, CMP_K, KV_B * CMP_HID).astype(BF16),
            w2bd.reshape(2, KV_B * CMP_HID, KV_B * 2 * HD).astype(BF16))


def _compress(cmp_rows, B, T, pe_flat, w1bd, w2bd):
    nblk = T // CMP_STRIDE
    return pl.pallas_call(
        functools.partial(_compress_kernel, T=T),
        grid=(B, 2),
        in_specs=[
            pl.BlockSpec((T, LANES), lambda b, k: (b, k)),
            pl.BlockSpec((None, 1, CMP_K), lambda b, k: (k, 0, 0)),
            pl.BlockSpec((None, CMP_K, 2 * CMP_HID), lambda b, k: (k, 0, 0)),
            pl.BlockSpec((None, 2 * CMP_HID, 2 * LANES), lambda b, k: (k, 0, 0)),
        ],
        out_specs=pl.BlockSpec((None, None, KV_B, nblk, LANES), lambda b, k: (b, k, 0, 0, 0)),
        out_shape=jax.ShapeDtypeStruct((B, 2, KV_B, nblk, LANES), BF16),
        scratch_shapes=[pltpu.VMEM((T + CMP_STRIDE, LANES), F32), pltpu.VMEM((nblk, CMP_K), BF16)],
        compiler_params=_cparams(("parallel", "parallel")),
        name="compress",
    )(cmp_rows, pe_flat, w1bd, w2bd)


def _group_head(g, rb):
    return R_B * g + 2 * (rb % 4) + rb // 4


def _stack_query(q4):
    lane = lax.broadcasted_iota(jnp.int32, q4.shape, 1)
    zero = jnp.zeros_like(q4)
    return jnp.concatenate([jnp.where(lane < HD, q4, zero), jnp.where(lane >= HD, q4, zero)], axis=0)


def _top_blocks(score, n):
    lanef = lax.broadcasted_iota(jnp.int32, score.shape, 1).astype(F32)
    width = float(score.shape[1])

    def body(_, carry):
        sc, sel = carry
        mx = jnp.max(sc, axis=1, keepdims=True)
        idx = jnp.min(jnp.where(sc == mx, lanef, width), axis=1, keepdims=True)
        hit = lanef == idx
        return jnp.where(hit, REMOVED, sc), jnp.where(hit, 1.0, sel)

    return lax.fori_loop(0, n, body, (score, jnp.zeros_like(score)))[1]


def _nsa_kernel(q_ref, gb_ref, kcv_ref, kselt_ref, kwint_ref, vsel_ref, vwin_ref, d_ref, ac_ref,
                c2s_ref, ex_ref, o_ref, s_ref, m_ref, l_ref, acc_ref, *, tq, tk, T):
    qi = pl.program_id(1)
    p0 = qi * tq
    nc = kcv_ref.shape[2]
    rows = R_B * tq
    half = rows // 2
    qpos = p0 + lax.broadcasted_iota(jnp.int32, (rows, 1), 0) % tq
    qpos_h = qpos[:half]
    lane4 = lax.broadcasted_iota(jnp.int32, (half, LANES), 1)

    gates = gb_ref[...]
    g_hi = gates.astype(BF16)
    g_lo = (gates - g_hi.astype(F32)).astype(BF16)
    gexp = [_dot(g_hi, ex_ref[br]) + _dot(g_lo, ex_ref[br]) for br in range(3)]

    def add_tile(hf, kind, off):
        for k in range(4):
            s_ref[k * tq:(k + 1) * tq, pl.ds(off, tq)] += d_ref[_group_head(g, 4 * hf + k), kind]

    st = p0 // CMP_STRIDE - BAND_LEAD
    ci = lax.broadcasted_iota(jnp.int32, (nc, LANES), 0)
    wl = lax.broadcasted_iota(jnp.int32, (nc, LANES), 1)
    place = jnp.where((ci == st + wl % BAND_W) & (wl < 2 * BAND_W), 1.0, 0.0).astype(BF16)
    cend = lax.broadcasted_iota(jnp.int32, (1, nc), 1) * CMP_STRIDE + (CMP_LEN - 1)
    mask = cend <= qpos
    blk = lax.broadcasted_iota(jnp.int32, (tq, LANES), 1)
    cur = (p0 + lax.broadcasted_iota(jnp.int32, (tq, 1), 0)) // SEL_BLOCK
    forced = (blk == 0) | (blk == cur) | (blk == cur - 1)
    Qs, o_cs, scores = [], [], []
    for g in range(KV_B):
        Q = _stack_query(q_ref[4 * g:4 * g + 4].reshape(4 * tq, LANES))
        s = _dot_nt(jnp.concatenate([Q, ac_ref[g]], axis=1), jnp.concatenate([kcv_ref[0, g], place], axis=1))
        s = jnp.where(mask, s, NEG)
        e = jnp.where(mask, jnp.exp(s - jnp.max(s, axis=-1, keepdims=True)), 0.0)
        pc = (e / jnp.maximum(jnp.sum(e, axis=-1, keepdims=True), 1e-30)).astype(BF16)
        oc = _dot(pc, kcv_ref[1, g])
        imp = jnp.sum(_dot(pc, c2s_ref[...]).reshape(R_B, tq, LANES), axis=0)
        Qs.append(Q)
        o_cs.append(jnp.where(lane4 < HD, oc[:half], oc[half:]))
        scores.append(jnp.where(blk <= cur, jnp.where(forced, FORCE, imp), NEG))
    sel = _top_blocks(jnp.concatenate(scores, axis=0), SEL_TOPN)

    for g in range(KV_B):
        o_c = o_cs[g]
        selneg = jnp.where((sel[g * tq:(g + 1) * tq] > 0.5) & (blk <= cur), 0.0, NEG).astype(BF16)
        Q = jnp.concatenate([Qs[g], jnp.concatenate([selneg] * R_B, axis=0)], axis=1)

        m_ref[...] = jnp.full(m_ref.shape, NEG, F32)
        l_ref[...] = jnp.zeros(l_ref.shape, F32)
        acc_ref[...] = jnp.zeros(acc_ref.shape, F32)
        bi = lax.broadcasted_iota(jnp.int32, (LANES, tk), 0)
        kb = lax.broadcasted_iota(jnp.int32, (LANES, tk), 1) // SEL_BLOCK

        def sel_block(j, mode):
            ks = pl.multiple_of(j * tk, tk)
            expand = jnp.where(bi - kb == j * (tk // SEL_BLOCK), 1.0, 0.0).astype(BF16)
            kt = jnp.concatenate([kselt_ref[g, :, pl.ds(ks, tk)], expand], axis=0)
            v = vsel_ref[g, pl.ds(ks, tk), :]
            for hf in range(2):
                s = _dot(Q[hf * half:(hf + 1) * half], kt)
                if mode:
                    s_ref[:, 0:tk] = s
                    if mode == 1:
                        add_tile(hf, 1, tk - tq)
                    else:
                        off0 = p0 - ks
                        add_tile(hf, 0, pl.multiple_of(off0, tq))

                        @pl.when(off0 >= tq)
                        def _():
                            add_tile(hf, 1, pl.multiple_of(off0 - tq, tq))

                    s = s_ref[:, 0:tk]
                    if mode == 2:
                        key = ks + lax.broadcasted_iota(jnp.int32, (1, tk), 1)
                        s = jnp.where(key <= qpos_h, s, NEG)
                _softmax_step(s, v, m_ref, l_ref, acc_ref, (0, slice(hf * half, (hf + 1) * half)))

        jd = p0 // tk
        at_block_start = (p0 % tk == 0) & (jd >= 1)

        def far_body(j, carry):
            sel_block(j, 0)
            return carry

        lax.fori_loop(0, jd - at_block_start.astype(jnp.int32), far_body, 0)

        @pl.when(at_block_start)
        def _():
            sel_block(jd - 1, 1)

        sel_block(jd, 2)
        o = acc_ref[0] / jnp.maximum(l_ref[0], 1e-30)
        o_s = jnp.where(lane4 < HD, o[:half], o[half:])

        ws = pl.multiple_of(jnp.maximum(p0 - WINDOW, 0), tq)
        ww = WINDOW + tq
        kt = kwint_ref[g, :, pl.ds(ws, ww)]
        v = vwin_ref[g, pl.ds(ws, ww), :]
        off0 = p0 - ws
        dist = qpos_h - (ws + lax.broadcasted_iota(jnp.int32, (1, ww), 1))
        mask = (dist >= 0) & (dist <= WINDOW)
        o_half = []
        for hf in range(2):
            s_ref[...] = _dot(Qs[g][hf * half:(hf + 1) * half], kt)
            add_tile(hf, 0, pl.multiple_of(off0, tq))

            @pl.when(off0 >= tq)
            def _():
                add_tile(hf, 1, pl.multiple_of(off0 - tq, tq))

            s = jnp.where(mask, s_ref[...], NEG)
            e = jnp.where(mask, jnp.exp(s - jnp.max(s, axis=-1, keepdims=True)), 0.0)
            o_half.append(_dot(e.astype(BF16), v) / jnp.maximum(jnp.sum(e, axis=-1, keepdims=True), 1e-30))
        o_w = jnp.where(lane4 < HD, o_half[0], o_half[1])

        for p in range(4):
            c0 = (4 * g + p) * LANES
            r = slice(p * tq, (p + 1) * tq)
            o_ref[:, c0:c0 + LANES] = (gexp[0][:, c0:c0 + LANES] * o_c[r] + gexp[1][:, c0:c0 + LANES] * o_s[r]
                                       + gexp[2][:, c0:c0 + LANES] * o_w[r]).astype(BF16)


def _nsa_tables(bias_b, tq, nc, ns_pad):
    far = bias_b[N_BUCKETS - 1]
    i = np.arange(tq)[:, None]
    tiles = _near_tiles(bias_b, tq, masked=False)
    w = np.arange(BAND_W)[None, :]
    dist = i + CMP_STRIDE * BAND_LEAD - CMP_STRIDE * w - (CMP_LEN - 1)
    assert dist[:, 0].min() >= FAR_DIST and dist[:, -1].max() < 0
    band = jnp.where(jnp.asarray((dist >= 0) & (dist < FAR_DIST)),
                     _bias_of_dist(bias_b, dist) - far[:, None, None], 0.0)
    order = [[_group_head(g, rb) for rb in range(R_B)] for g in range(KV_B)]
    band = jnp.stack([jnp.concatenate([band[h] for h in order[g]], axis=0) for g in range(KV_B)])
    b_hi = band.astype(BF16)
    b_lo = (band - b_hi.astype(F32)).astype(BF16)
    ac = jnp.concatenate([b_hi, b_lo, jnp.zeros(band.shape[:2] + (LANES - 2 * BAND_W,), BF16)], axis=2)
    ci = np.arange(nc)[:, None] * CMP_STRIDE
    lo = np.arange(ns_pad)[None, :] * SEL_BLOCK
    c2s = jnp.asarray(((ci < lo + SEL_BLOCK) & (ci + CMP_LEN > lo)).astype(np.float32)).astype(BF16)
    k = np.arange(LANES)[:, None]
    col = np.arange(H_B * HD)[None, :]
    ex = jnp.asarray(np.stack([(k == (col // HD) * 3 + br) for br in range(3)]).astype(np.float32)).astype(BF16)
    return tiles, ac, c2s, ex


def _nsa(qb4, gb, kcv, kdupt, vdup, bias_b, tq, tk):
    B, _, T, _ = qb4.shape
    nq = T // tq
    nc = kcv.shape[3]
    assert T // SEL_BLOCK <= LANES and T >= WINDOW + tq and T % tk == 0
    tiles, ac, c2s, ex = _nsa_tables(bias_b, tq, nc, LANES)
    const = lambda *shape: pl.BlockSpec(shape, lambda b, i: (0,) * len(shape))
    return pl.pallas_call(
        functools.partial(_nsa_kernel, tq=tq, tk=tk, T=T),
        grid=(B, nq),
        in_specs=[
            pl.BlockSpec((None, H_B // 2, tq, LANES), lambda b, i: (b, 0, i, 0)),
            pl.BlockSpec((tq, LANES), lambda b, i: (b * nq + i, 0)),
            _resident((None, 2, KV_B, nc, LANES), lambda b, i: (b, 0, 0, 0, 0)),
            _resident((None, KV_B, LANES, T), lambda b, i: (b, 0, 0, 0)),
            _resident((None, KV_B, LANES, T), lambda b, i: (b, 1, 0, 0)),
            _resident((None, KV_B, T, LANES), lambda b, i: (b, 0, 0, 0)),
            _resident((None, KV_B, T, LANES), lambda b, i: (b, 1, 0, 0)),
            _resident((H_B, 2, tq, tq), lambda b, i: (0, 0, 0, 0)),
            _resident((KV_B, R_B * tq, LANES), lambda b, i: (0, 0, 0)),
            _resident((nc, LANES), lambda b, i: (0, 0)),
            _resident((3, LANES, H_B * HD), lambda b, i: (0, 0, 0)),
        ],
        out_specs=pl.BlockSpec((tq, H_B * HD), lambda b, i: (b * nq + i, 0)),
        out_shape=jax.ShapeDtypeStruct((B * T, H_B * HD), BF16),
        scratch_shapes=[pltpu.VMEM((R_B * tq // 2, WINDOW + tq), F32)] + [pltpu.VMEM((1, R_B * tq, LANES), F32)] * 3,
        compiler_params=_cparams(("parallel", "arbitrary")),
        name="nsa",
    )(qb4, gb, kcv, kdupt, kdupt, vdup, vdup, tiles, ac, c2s, ex)


def _merge_kernel(x_ref, g_ref, a_ref, b_ref, wgm_ref, wa_ref, wb_ref, wo_ref, o_ref):
    D = x_ref.shape[1]
    x = x_ref[...]
    hb = _rms(x, g_ref[...]).astype(BF16)
    ya = jax.nn.sigmoid(_dot(hb, wgm_ref[:, 0:D])) * _dot(a_ref[...], wa_ref[...])
    yb = jax.nn.sigmoid(_dot(hb, wgm_ref[:, D:2 * D])) * _dot(b_ref[...], wb_ref[...])
    o_ref[...] = x + _dot((ya + yb).astype(BF16), wo_ref[...])


def _merge(x2, gain, a, ob, w_gm, w_a, w_b, w_o, tm):
    n, D = x2.shape
    row = lambda i: (i, 0)
    return pl.pallas_call(
        _merge_kernel,
        grid=(n // tm,),
        in_specs=[
            pl.BlockSpec((tm, D), row),
            pl.BlockSpec((1, D), lambda i: (0, 0)),
            pl.BlockSpec((tm, a.shape[1]), row),
            pl.BlockSpec((tm, ob.shape[1]), row),
            _resident(w_gm.shape, lambda i: (0, 0)),
            _resident(w_a.shape, lambda i: (0, 0)),
            _resident(w_b.shape, lambda i: (0, 0)),
            _resident(w_o.shape, lambda i: (0, 0)),
        ],
        out_specs=pl.BlockSpec((tm, D), row),
        out_shape=jax.ShapeDtypeStruct((n, D), F32),
        compiler_params=_cparams(("parallel",)),
        name="merge",
    )(x2, gain.reshape(1, D), a, ob, w_gm, w_a, w_b, w_o)


def _swiglu_chunk(hb, w1_ref, w3_ref, w2_ref):
    t = jax.nn.silu(_dot(hb, w1_ref[...])) * _dot(hb, w3_ref[...])
    return _dot(t.astype(BF16), w2_ref[...])


def _ffn_kernel(x_ref, g_ref, w1_ref, w3_ref, w2_ref, gf_ref, o_ref, hb_ref, acc_ref, *, final_norm):
    f = pl.program_id(1)

    @pl.when(f == 0)
    def _():
        hb_ref[...] = _rms(x_ref[...], g_ref[...]).astype(BF16)
        acc_ref[...] = jnp.zeros(acc_ref.shape, F32)

    acc_ref[...] += _swiglu_chunk(hb_ref[...], w1_ref, w3_ref, w2_ref)

    @pl.when(f == pl.num_programs(1) - 1)
    def _():
        y = x_ref[...] + acc_ref[...]
        o_ref[...] = _rms(y, gf_ref[...]) if final_norm else y


def _ffn(x2, gain, w1, w3, w2, gain_final, final_norm, tm, tf):
    n, D = x2.shape
    F = w1.shape[1]
    return pl.pallas_call(
        functools.partial(_ffn_kernel, final_norm=final_norm),
        grid=(n // tm, F // tf),
        in_specs=[
            pl.BlockSpec((tm, D), lambda i, f: (i, 0)),
            pl.BlockSpec((1, D), lambda i, f: (0, 0)),
            pl.BlockSpec((D, tf), lambda i, f: (0, f)),
            pl.BlockSpec((D, tf), lambda i, f: (0, f)),
            pl.BlockSpec((tf, D), lambda i, f: (f, 0)),
            pl.BlockSpec((1, D), lambda i, f: (0, 0)),
        ],
        out_specs=pl.BlockSpec((tm, D), lambda i, f: (i, 0)),
        out_shape=jax.ShapeDtypeStruct((n, D), F32),
        scratch_shapes=[pltpu.VMEM((tm, D), BF16), pltpu.VMEM((tm, D), F32)],
        compiler_params=_cparams(("parallel", "arbitrary")),
        name="ffn",
    )(x2, gain.reshape(1, D), w1, w3, w2, gain_final.reshape(1, D))


def _moe_kernel(x_ref, g_ref, wr_ref, w1_ref, w3_ref, w2_ref, gf_ref, o_ref, hb_ref, gate_ref, acc_ref, *,
                final_norm):
    e = pl.program_id(1)
    f = pl.program_id(2)

    @pl.when((e == 0) & (f == 0))
    def _():
        hb = _rms(x_ref[...], g_ref[...]).astype(BF16)
        hb_ref[...] = hb
        acc_ref[...] = jnp.zeros(acc_ref.shape, F32)
        logits = _dot(hb, wr_ref[...])
        lane = lax.broadcasted_iota(jnp.int32, logits.shape, 1)
        lanef = lane.astype(F32)
        logits = jnp.where(lane < N_EXPERTS, logits, NEG)
        v1 = jnp.max(logits, axis=-1, keepdims=True)
        i1 = jnp.min(jnp.where(logits == v1, lanef, float(LANES)), axis=-1, keepdims=True)
        rest = jnp.where(lanef == i1, REMOVED, logits)
        v2 = jnp.max(rest, axis=-1, keepdims=True)
        i2 = jnp.min(jnp.where(rest == v2, lanef, float(LANES)), axis=-1, keepdims=True)
        e2 = jnp.exp(v2 - v1)
        w_1 = 1.0 / (1.0 + e2)
        w_2 = e2 / (1.0 + e2)
        gate_ref[...] = jnp.where(lanef == i1, w_1, 0.0) + jnp.where(lanef == i2, w_2, 0.0)

    lane = lax.broadcasted_iota(jnp.int32, gate_ref.shape, 1)
    gate_e = jnp.sum(jnp.where(lane == e, gate_ref[...], 0.0), axis=-1, keepdims=True)
    acc_ref[...] += gate_e * _swiglu_chunk(hb_ref[...], w1_ref, w3_ref, w2_ref)

    @pl.when((e == pl.num_programs(1) - 1) & (f == pl.num_programs(2) - 1))
    def _():
        y = x_ref[...] + acc_ref[...]
        o_ref[...] = _rms(y, gf_ref[...]) if final_norm else y


def _moe(x2, gain, w_r, w1, w3, w2, gain_final, final_norm, tm, tf):
    n, D = x2.shape
    E, _, F = w1.shape
    return pl.pallas_call(
        functools.partial(_moe_kernel, final_norm=final_norm),
        grid=(n // tm, E, F // tf),
        in_specs=[
            pl.BlockSpec((tm, D), lambda i, e, f: (i, 0)),
            pl.BlockSpec((1, D), lambda i, e, f: (0, 0)),
            pl.BlockSpec((D, LANES), lambda i, e, f: (0, 0)),
            pl.BlockSpec((None, D, tf), lambda i, e, f: (e, 0, f)),
            pl.BlockSpec((None, D, tf), lambda i, e, f: (e, 0, f)),
            pl.BlockSpec((None, tf, D), lambda i, e, f: (e, f, 0)),
            pl.BlockSpec((1, D), lambda i, e, f: (0, 0)),
        ],
        out_specs=pl.BlockSpec((tm, D), lambda i, e, f: (i, 0)),
        out_shape=jax.ShapeDtypeStruct((n, D), F32),
        scratch_shapes=[pltpu.VMEM((tm, D), BF16), pltpu.VMEM((tm, LANES), F32), pltpu.VMEM((tm, D), F32)],
        compiler_params=_cparams(("parallel", "arbitrary", "arbitrary")),
        name="moe",
    )(x2, gain.reshape(1, D), w_r, w1, w3, w2, gain_final.reshape(1, D))


SROWS = 16


def _page_specs(block, kind_block, layer, n_fixed):
    tail = (0,) * (len(block) - 3)

    def spec(j):
        def index_map(b, c, pt, *_):
            return (layer, pt[b, c * PAGES_PER_STEP + j], kind_block) + tail
        return pl.BlockSpec(block, index_map)

    del n_fixed
    return [spec(j) for j in range(PAGES_PER_STEP)]


def _stack_rows(top, bot):
    pad = jnp.zeros((SROWS - top.shape[0] - bot.shape[0], LANES), F32)
    return jnp.concatenate([top, bot, pad], axis=0)


def _adec_kernel(pt_ref, q_ref, ks_ref, vs_ref, blast_ref, b0_ref, lam_ref, sub_ref, *rest, lam_init):
    pages = rest[:PAGES_PER_STEP]
    o_ref, qs_ref, m_ref, l_ref, acc_ref = rest[PAGES_PER_STEP:]
    c = pl.program_id(1)
    last = pl.num_programs(1) - 1
    lane2 = lax.broadcasted_iota(jnp.int32, (2, LANES), 1)

    @pl.when(c == 0)
    def _():
        for g in range(KV_A):
            qq = q_ref[2 * g:2 * g + 2]
            qg = _stack_rows(jnp.where(lane2 < HD, qq, 0.0), jnp.where(lane2 >= HD, qq, 0.0))
            qs_ref[g] = qg
            k_self = ks_ref[g:g + 1].astype(BF16).astype(F32)
            s_self = jnp.sum(qg.astype(BF16).astype(F32) * k_self, axis=-1, keepdims=True)
            m_ref[g] = s_self + b0_ref[g]
            l_ref[g] = jnp.ones((SROWS, LANES), F32)
            acc_ref[g] = jnp.broadcast_to(vs_ref[g:g + 1].astype(BF16).astype(F32), (SROWS, LANES))

    for g in range(KV_A):
        kg = jnp.concatenate([p[pl.ds(g, PAGE, stride=2 * KV_A), :] for p in pages], axis=0).astype(BF16)
        vg = jnp.concatenate([p[pl.ds(KV_A + g, PAGE, stride=2 * KV_A), :] for p in pages], axis=0).astype(BF16)
        s = _dot_nt(qs_ref[g].astype(BF16), kg)
        s = s + jnp.where(c == last, blast_ref[g], 0.0)
        _softmax_step(s, vg, m_ref, l_ref, acc_ref, g)

    @pl.when(c == last)
    def _():
        for g in range(KV_A):
            o = acc_ref[g] / jnp.maximum(l_ref[g], 1e-30)
            o_ref[2 * g:2 * g + 2, :] = _diff_finish(o[0:2], o[2:4], lam_ref[...], sub_ref[...], lam_init)


def _head_rows_a(vals):
    out = []
    for g in range(KV_A):
        r = [vals[2 * g], vals[2 * g + 1], vals[2 * g], vals[2 * g + 1]]
        out.append(jnp.stack(r + [jnp.zeros_like(vals[0])] * (SROWS - 4)))
    return jnp.stack(out)


def _head_rows_b(vals):
    out = []
    for g in range(KV_B):
        r = [vals[_group_head(g, rb)] for rb in range(R_B)]
        out.append(jnp.stack(r + [jnp.zeros_like(vals[0])] * (SROWS - R_B)))
    return jnp.stack(out)


def _adec(page_table, cache_view, layer, q, k_self, v_self, bias_a, lam_p, subln, lam_init):
    DB, NP = page_table.shape
    far = bias_a[N_BUCKETS - 1][:, None]
    kw = PAGES_PER_STEP * PAGE
    near = _bias_of_dist(bias_a, PAGE - np.arange(PAGE)) - far
    blast = _head_rows_a(jnp.pad(near, ((0, 0), (kw - PAGE, 0))))
    b0 = _head_rows_a((bias_a[0][:, None] - far) * jnp.ones((1, LANES), F32))
    per_b = lambda *blk: pl.BlockSpec((None,) + blk, lambda b, c, pt: (b,) + (0,) * len(blk))
    const = lambda *blk: pl.BlockSpec(blk, lambda b, c, pt: (0,) * len(blk))
    return pl.pallas_call(
        functools.partial(_adec_kernel, lam_init=lam_init),
        grid_spec=pltpu.PrefetchScalarGridSpec(
            num_scalar_prefetch=1,
            grid=(DB, NP // PAGES_PER_STEP),
            in_specs=[per_b(H_A, LANES), per_b(KV_A, LANES), per_b(KV_A, LANES),
                      const(KV_A, SROWS, kw), const(KV_A, SROWS, LANES), const(4, HD), const(1, 2 * HD)]
            + _page_specs((None, None, PAGE * 2 * KV_A, LANES), 0, layer, 0),
            out_specs=per_b(H_A, LANES),
            scratch_shapes=[pltpu.VMEM((KV_A, SROWS, LANES), F32)] * 4,
        ),
        out_shape=jax.ShapeDtypeStruct((DB, H_A, LANES), F32),
        compiler_params=_cparams(("parallel", "arbitrary")),
        name="adec",
    )(page_table, q, k_self, v_self, blast, b0, lam_p, subln.reshape(1, 2 * HD),
      *([cache_view] * PAGES_PER_STEP))


def _stack_pairs(q4):
    lane = lax.broadcasted_iota(jnp.int32, q4.shape, 1)
    return _stack_rows(jnp.where(lane < HD, q4, 0.0), jnp.where(lane >= HD, q4, 0.0))


def _top_indices(score, n):
    lanef = lax.broadcasted_iota(jnp.int32, score.shape, 1).astype(F32)
    slot = lax.broadcasted_iota(jnp.int32, (score.shape[0], LANES), 1)
    width = float(score.shape[1])

    def body(it, carry):
        sc, out = carry
        mx = jnp.max(sc, axis=1, keepdims=True)
        idx = jnp.min(jnp.where(sc == mx, lanef, width), axis=1, keepdims=True)
        return jnp.where(lanef == idx, REMOVED, sc), jnp.where(slot == it, idx.astype(jnp.int32), out)

    return lax.fori_loop(0, n, body, (score, jnp.zeros((score.shape[0], LANES), jnp.int32)))[1]


def _ncmp_kernel(pt_ref, q_ref, new_ref, bc_ref, pe_ref, w1_ref, w2_ref, c2s_ref, *rest, past):
    pages = rest[:PAGES_PER_STEP]
    oc_ref, idx_ref, xk_ref, xv_ref, p_ref = rest[PAGES_PER_STEP:]
    c = pl.program_id(1)
    stage = (xk_ref, xv_ref)
    for j, page in enumerate(pages):
        r0 = pl.multiple_of((c * PAGES_PER_STEP + j) * PAGE, PAGE)
        for kind in range(2):
            stage[kind][pl.ds(r0, PAGE), :] = page[kind].reshape(KV_B * HD, PAGE).T

    @pl.when(c == pl.num_programs(1) - 1)
    def _():
        nblk = past // CMP_STRIDE
        trow = lax.broadcasted_iota(jnp.int32, (2 * CMP_STRIDE, LANES), 0)
        res = []
        for kind in range(2):
            stage[kind][past:past + 2 * CMP_STRIDE, :] = jnp.where(trow == 0, new_ref[kind], 0.0)
            res.append(_compress_core(stage[kind], nblk, pe_ref.at[kind], w1_ref.at[kind], w2_ref.at[kind],
                                      p_ref).astype(BF16))
        ns_pad = c2s_ref.shape[1]
        for g in range(KV_B):
            qg = _stack_pairs(q_ref[4 * g:4 * g + 4]).astype(BF16)
            s = _dot_nt(qg, res[0][:, g * LANES:(g + 1) * LANES]) + bc_ref[g]
            e = jnp.exp(s - jnp.max(s, axis=-1, keepdims=True))
            pc = (e / jnp.maximum(jnp.sum(e, axis=-1, keepdims=True), 1e-30)).astype(BF16)
            oc_ref[g] = _dot(pc, res[1][:, g * LANES:(g + 1) * LANES])
            imp = jnp.sum(_dot(pc, c2s_ref[...])[0:R_B], axis=0, keepdims=True)
            blk = lax.broadcasted_iota(jnp.int32, (R_B, ns_pad), 1)
            cur = past // SEL_BLOCK
            forced = (blk == 0) | (blk == cur) | (blk == cur - 1)
            score = jnp.where(blk <= cur, jnp.where(forced, FORCE, jnp.broadcast_to(imp, (R_B, ns_pad))), NEG)
            idx_ref[g] = _top_indices(score, SEL_TOPN)


def _ncmp(page_table, nsa_view, layer, qb, new_cmp, bias_b, pe_flat, w1bd, w2bd):
    DB, NP = page_table.shape
    past = NP * PAGE
    nblk = past // CMP_STRIDE
    ns = past // SEL_BLOCK + 1
    ns_pad = -(-ns // LANES) * LANES
    i = np.arange(nblk)
    dist = past - (i * CMP_STRIDE + CMP_LEN - 1)
    bc = _head_rows_b(jnp.where(jnp.asarray(dist >= 0), _bias_of_dist(bias_b, dist), NEG))
    ci = i[:, None] * CMP_STRIDE
    lo = np.arange(ns_pad)[None, :] * SEL_BLOCK
    c2s = jnp.asarray(((ci < lo + SEL_BLOCK) & (ci + CMP_LEN > lo) & (lo < ns * SEL_BLOCK)).astype(np.float32))
    per_b = lambda *blk: pl.BlockSpec((None,) + blk, lambda b, c, pt: (b,) + (0,) * len(blk))
    const = lambda *blk: pl.BlockSpec(blk, lambda b, c, pt: (0,) * len(blk), pipeline_mode=pl.Buffered(1))
    return pl.pallas_call(
        functools.partial(_ncmp_kernel, past=past),
        grid_spec=pltpu.PrefetchScalarGridSpec(
            num_scalar_prefetch=1,
            grid=(DB, NP // PAGES_PER_STEP),
            in_specs=[per_b(H_B // 2, LANES), per_b(2, 1, LANES), const(KV_B, SROWS, nblk),
                      const(2, 1, CMP_K), const(2, CMP_K, 2 * CMP_HID), const(2, 2 * CMP_HID, 2 * LANES),
                      const(nblk, ns_pad)]
            + _page_specs((None, None, 2, KV_B, HD, PAGE), 0, layer, 0),
            out_specs=[per_b(KV_B, SROWS, LANES), per_b(KV_B, R_B, LANES)],
            scratch_shapes=[pltpu.VMEM((past + 2 * CMP_STRIDE, LANES), F32),
                            pltpu.VMEM((past + 2 * CMP_STRIDE, LANES), F32),
                            pltpu.VMEM((nblk, CMP_K), BF16)],
        ),
        out_shape=[jax.ShapeDtypeStruct((DB, KV_B, SROWS, LANES), F32),
                   jax.ShapeDtypeStruct((DB, KV_B, R_B, LANES), jnp.int32)],
        compiler_params=_cparams(("parallel", "arbitrary")),
        name="ncmp",
    )(page_table, qb, new_cmp, bc, pe_flat, w1bd, w2bd, c2s.astype(BF16), *([nsa_view] * PAGES_PER_STEP))


def _dup_t(x):
    return jnp.concatenate([x, x], axis=0).astype(BF16)


def _nsel_kernel(pt_ref, idx_ref, q_ref, gb_ref, oc_ref, new_ref, newcol_ref, blast_ref, b0_ref,
                 bw_ref, win_ref, ex_ref, *rest, n_pages):
    sel_pages = rest[:KV_B * SEL_TOPN]
    o_ref, wout_ref = rest[KV_B * SEL_TOPN:]
    b = pl.program_id(0)
    lane = lax.broadcasted_iota(jnp.int32, (SROWS, LANES), 1)
    lane4 = lax.broadcasted_iota(jnp.int32, (4, LANES), 1)
    g8 = jnp.broadcast_to(gb_ref[...], (8, LANES))
    g_hi = g8.astype(BF16)
    g_lo = (g8 - g_hi.astype(F32)).astype(BF16)
    gexp = [_dot(g_hi, ex_ref[br]) + _dot(g_lo, ex_ref[br]) for br in range(3)]

    def unstack(o):
        return jnp.where(lane4 < HD, o[0:4], o[4:8])

    def rounded(x):
        return x.astype(BF16).astype(F32)

    for g in range(KV_B):
        qg = _stack_pairs(q_ref[4 * g:4 * g + 4])
        qb = qg.astype(BF16)
        qr = rounded(qg)

        def self_score(k):
            return jnp.sum(qr * rounded(new_ref[k:k + 1]), axis=-1, keepdims=True) + b0_ref[g][:, 0:1]

        s_self = self_score(g)
        scores = []
        for t in range(SEL_TOPN):
            blk = idx_ref[b, g * SEL_TOPN + t]
            page = sel_pages[g * SEL_TOPN + t]
            s = _dot(qb, _dup_t(page[0, g]))
            s = s + jnp.where(blk // 2 == n_pages - 1, blast_ref[g], 0.0)
            ok = (blk < 2 * n_pages) & ((lane // SEL_BLOCK) == blk % 2)
            scores.append(jnp.where(ok, s, NEG))
        m = s_self
        for s in scores:
            m = jnp.maximum(m, jnp.max(s, axis=-1, keepdims=True))
        e_self = jnp.exp(s_self - m)
        l = e_self
        acc = rounded(e_self) * rounded(new_ref[4 + g:5 + g])
        for t, s in enumerate(scores):
            e = jnp.exp(s - m)
            l = l + jnp.sum(e, axis=-1, keepdims=True)
            acc = acc + _dot_nt(e.astype(BF16), _dup_t(sel_pages[g * SEL_TOPN + t][1, g]))
        o_s = unstack(acc / jnp.maximum(l, 1e-30))

        s_self = self_score(2 + g)
        s = _dot(qb, _dup_t(win_ref[0, g])) + bw_ref[g]
        m = jnp.maximum(s_self, jnp.max(s, axis=-1, keepdims=True))
        e = jnp.exp(s - m)
        e_self = jnp.exp(s_self - m)
        l = e_self + jnp.sum(e, axis=-1, keepdims=True)
        acc = rounded(e_self) * rounded(new_ref[6 + g:7 + g]) + _dot_nt(e.astype(BF16), _dup_t(win_ref[1, g]))
        o_w = unstack(acc / jnp.maximum(l, 1e-30))
        o_c = unstack(oc_ref[g])

        for p in range(4):
            c0 = (4 * g + p) * LANES
            o_ref[:, c0:c0 + LANES] = (gexp[0][0:1, c0:c0 + LANES] * o_c[p:p + 1]
                                       + gexp[1][0:1, c0:c0 + LANES] * o_s[p:p + 1]
                                       + gexp[2][0:1, c0:c0 + LANES] * o_w[p:p + 1])

    n_win = win_ref.shape[3]
    wl = lax.broadcasted_iota(jnp.int32, (HD, n_win), 1)
    for kv in range(2):
        for g in range(KV_B):
            r0 = (kv * KV_B + g) * HD
            shifted = pltpu.roll(win_ref[kv, g], n_win - 1, axis=1)
            wout_ref[kv, g] = jnp.where(wl == n_win - 1, newcol_ref[r0:r0 + HD, :], shifted)


def _nsel(page_table, idx, nsa_view, win_view, layer, qb, gates, oc, new_rows, new_col, bias_b):
    DB, NP = page_table.shape
    n_win = win_view.shape[-1]
    far = bias_b[N_BUCKETS - 1][:, None]
    blast = _head_rows_b(_bias_of_dist(bias_b, PAGE - np.arange(PAGE)) - far)
    b0 = _head_rows_b((bias_b[0][:, None] - far) * jnp.ones((1, LANES), F32))
    bw = _head_rows_b(_bias_of_dist(bias_b, n_win - np.arange(n_win)) - far)
    k = np.arange(LANES)[:, None]
    col = np.arange(H_B * HD)[None, :]
    ex = jnp.asarray(np.stack([(k == (col // HD) * 3 + br) for br in range(3)]).astype(np.float32)).astype(BF16)
    per_b = lambda *blk: pl.BlockSpec((None,) + blk, lambda b, pt, ix: (b,) + (0,) * len(blk))
    const = lambda *blk: pl.BlockSpec(blk, lambda b, pt, ix: (0,) * len(blk))

    def sel_spec(g, t):
        def index_map(b, pt, ix):
            return (layer, pt[b, jnp.clip(ix[b, g * SEL_TOPN + t] // 2, 0, NP - 1)], 1, 0, 0, 0)
        return pl.BlockSpec((None, None, 2, KV_B, HD, PAGE), index_map)

    return pl.pallas_call(
        functools.partial(_nsel_kernel, n_pages=NP),
        grid_spec=pltpu.PrefetchScalarGridSpec(
            num_scalar_prefetch=2,
            grid=(DB,),
            in_specs=[per_b(H_B // 2, LANES), per_b(1, LANES), per_b(KV_B, SROWS, LANES), per_b(8, LANES),
                      per_b(4 * HD, 1), const(KV_B, SROWS, LANES), const(KV_B, SROWS, LANES),
                      const(KV_B, SROWS, n_win),
                      pl.BlockSpec((None, None, 2, KV_B, HD, n_win), lambda b, pt, ix: (layer, b, 0, 0, 0, 0)),
                      const(3, LANES, H_B * HD)]
            + [sel_spec(g, t) for g in range(KV_B) for t in range(SEL_TOPN)],
            out_specs=[per_b(1, H_B * HD), per_b(2, KV_B, HD, n_win)],
        ),
        out_shape=[jax.ShapeDtypeStruct((DB, 1, H_B * HD), F32),
                   jax.ShapeDtypeStruct((DB, 2, KV_B, HD, n_win), F32)],
        compiler_params=_cparams(("parallel",)),
        name="nsel",
    )(page_table, idx, qb, gates, oc, new_rows, new_col, blast, b0, bw, win_view, ex,
      *([nsa_view] * (KV_B * SEL_TOPN)))


def _row_tile(n, cap):
    return n if n <= cap else cap


def _ff_tile(F):
    return F // 2 if (F > 1408 and (F // 2) % LANES == 0) else F


def kernel(x_prompt, x_sample, cache_attn_kv, cache_nsa_kv, state_win_kv, page_table, rel_bias, norm_mix, w_in, w_a, w_b, w_o, diff_lambda, diff_subln, cmp_pe, cmp_w1, cmp_w2, norm_ffn, ffn_w1, ffn_w3, ffn_w2, moe_router, moe_w1, moe_w3, moe_w2, norm_final):
    B, T, D = x_prompt.shape
    DB, dec_t, _ = x_sample.shape
    assert dec_t == 1
    depth = w_in.shape[0]
    n_pool = cache_attn_kv.shape[1]
    n_win = state_win_kv.shape[2]
    bias_a, bias_b = rel_bias[:, :H_A], rel_bias[:, H_A:]
    attn_view = cache_attn_kv.reshape(depth, n_pool, PAGE * 2 * KV_A, LANES)
    nsa_view = jnp.transpose(cache_nsa_kv, (0, 1, 3, 4, 5, 2))
    win_view = jnp.transpose(state_win_kv, (0, 1, 3, 4, 5, 2))

    xp = x_prompt.reshape(B * T, D)
    xs = x_sample.reshape(DB, D)
    tm = _row_tile(T, 512)
    att_t = _row_tile(T, ATT_T)
    outs = [[] for _ in range(6)]
    for l in range(depth):
        lam_init = 0.8 - 0.6 * math.exp(-0.3 * l)
        last = l == depth - 1
        w_row, w_t, w_gm = _prep_in_weights(w_in[l])
        wa, wb, wo = w_a[l].astype(BF16), w_b[l].astype(BF16), w_o[l].astype(BF16)
        pe_flat, w1bd, w2bd = _prep_cmp_weights(cmp_pe[l], cmp_w1[l], cmp_w2[l])

        qa4, kva, va, qb4, cmp_rows, vdup, gb, kat, nsat, wint, kdupt = _proj(
            xp.reshape(B, T, D), norm_mix[l], w_row, w_t, tm)
        a = _dattn(qa4, kat, va, bias_a, diff_lambda[l], diff_subln[l], lam_init, att_t)
        kcv = _compress(cmp_rows, B, T, pe_flat, w1bd, w2bd)
        ob = _nsa(qb4, gb, kcv, kdupt, vdup, bias_b, NSA_TQ, NSA_TK)
        xp = _merge(xp, norm_mix[l], a, ob, w_gm, wa, wb, wo, tm)
        outs[0].append(kva.reshape(B, T, 2, KV_A, 2 * HD))
        outs[2].append(jnp.transpose(nsat.reshape(B, 4, KV_B, HD, T), (0, 4, 1, 2, 3)))
        wlen = min(WINDOW, T)
        outs[4].append(jnp.transpose(wint[:, :, T - wlen:].reshape(B, 2, KV_B, HD, wlen), (0, 4, 1, 2, 3)))

        qa_s, kva_s, _, qb_s, cmp_s, vdup_s, gb_s, _, nsat_s, wint_s, kdupt_s = _proj(
            xs.reshape(1, DB, D), norm_mix[l], w_row, w_t, DB)
        kva_s = kva_s.reshape(DB, 2, KV_A, 2 * HD)
        q_a = jnp.transpose(qa_s[0], (1, 0, 2)).astype(F32)
        q_b = jnp.transpose(qb_s[0], (1, 0, 2)).astype(F32)
        a_s = _adec(page_table, attn_view, l, q_a, kva_s[:, 0], kva_s[:, 1], bias_a, diff_lambda[l],
                    diff_subln[l], lam_init)
        oc, idx = _ncmp(page_table, nsa_view, l, q_b, cmp_s.reshape(DB, 2, 1, LANES), bias_b, pe_flat, w1bd, w2bd)
        new_rows = jnp.concatenate([jnp.transpose(kdupt_s[0], (2, 0, 1)), jnp.transpose(vdup_s[0], (1, 0, 2))],
                                   axis=1).astype(F32)
        new_col = jnp.transpose(wint_s[0]).reshape(DB, 4 * HD, 1)
        ob_s, wout = _nsel(page_table, idx[:, :, 0, :SEL_TOPN].reshape(DB, KV_B * SEL_TOPN), nsa_view, win_view, l, q_b,
                           gb_s.reshape(DB, 1, LANES), oc, new_rows, new_col, bias_b)
        xs = _merge(xs, norm_mix[l], a_s.reshape(DB, H_A * 2 * HD).astype(BF16),
                    ob_s.reshape(DB, H_B * HD).astype(BF16), w_gm, wa, wb, wo, DB)
        outs[1].append(kva_s.reshape(DB, 1, 2, KV_A, 2 * HD))
        outs[3].append(jnp.transpose(nsat_s[0]).reshape(DB, 1, 4, KV_B, HD))
        outs[5].append(jnp.transpose(wout, (0, 4, 1, 2, 3)))

        e = l // 2
        if l % 2 == 0:
            w1, w3, w2 = ffn_w1[e].astype(BF16), ffn_w3[e].astype(BF16), ffn_w2[e].astype(BF16)
            tf = _ff_tile(w1.shape[1])
            xp = _ffn(xp, norm_ffn[l], w1, w3, w2, norm_final, last, tm, tf)
            xs = _ffn(xs, norm_ffn[l], w1, w3, w2, norm_final, last, DB, tf)
        else:
            w1, w3, w2 = moe_w1[e].astype(BF16), moe_w3[e].astype(BF16), moe_w2[e].astype(BF16)
            w_r = jnp.pad(moe_router[e], ((0, 0), (0, LANES - N_EXPERTS))).astype(BF16)
            tf = _ff_tile(w1.shape[2])
            xp = _moe(xp, norm_ffn[l], w_r, w1, w3, w2, norm_final, last, _row_tile(B * T, 1024), tf)
            xs = _moe(xs, norm_ffn[l], w_r, w1, w3, w2, norm_final, last, DB, tf)

    return (xp.reshape(B, T, D), xs.reshape(DB, 1, D)) + tuple(jnp.stack(o) for o in outs)
```

```python
import functools
import math

import numpy as np
import jax
import jax.numpy as jnp
from jax import lax
from jax.experimental import pallas as pl
from jax.experimental.pallas import tpu as pltpu

F32 = jnp.float32
BF16 = jnp.bfloat16

HD = 64
H_A = 8
KV_A = 4
H_B = 16
KV_B = 2
R_B = H_B // KV_B
CMP_LEN = 32
CMP_STRIDE = 16
CMP_HID = 128
SEL_BLOCK = 64
SEL_TOPN = 16
WINDOW = 512
N_BUCKETS = 32
MAX_DIST = 128
N_EXPERTS = 8
PAGE = 128
EPS = 1e-6
NEG = -1e30
FORCE = 1e9
REMOVED = -3e38

LANES = 128
VMEM_LIMIT = 56 * 1024 * 1024
MOE_VMEM_LIMIT = 58 * 1024 * 1024

ATT_T = 512
NSA_TQ = 128
NSA_TK = 512
BAND_W = 32
BAND_LEAD = 16
PAGES_PER_STEP = 16

N_ROW = 3968
N_T = 1792


def _cparams(sem):
    return pltpu.CompilerParams(dimension_semantics=sem, vmem_limit_bytes=VMEM_LIMIT)


def _resident(block, index_map):
    return pl.BlockSpec(block, index_map, pipeline_mode=pl.Buffered(1))


def _rms(x, g):
    xf = x.astype(F32)
    return xf * lax.rsqrt(jnp.mean(xf * xf, axis=-1, keepdims=True) + EPS) * g


def _dot(a, b):
    return jnp.dot(a, b, preferred_element_type=F32)


def _dot_nt(a, b):
    return lax.dot_general(a, b, (((1,), (1,)), ((), ())), preferred_element_type=F32)


def _bucket_np(d):
    d = np.maximum(np.asarray(d, np.int64), 0)
    exact = N_BUCKETS // 2
    df = np.maximum(d, 1).astype(np.float32)
    far = exact + (np.log(df / np.float32(exact)) / np.float32(math.log(MAX_DIST / exact))
                   * np.float32(N_BUCKETS - exact)).astype(np.int64)
    return np.where(d < exact, d, np.minimum(far, N_BUCKETS - 1))


FAR_DIST = 128
assert np.all(_bucket_np(np.arange(FAR_DIST, 1 << 16)) == N_BUCKETS - 1)


def _bias_of_dist(table, dist):
    idx = jnp.asarray(_bucket_np(dist).astype(np.int32))
    return jnp.moveaxis(jnp.take(table, idx, axis=0), -1, 0)


def _proj_kernel(x_ref, g_ref, wr_ref, wt_ref, qa_ref, kva_ref, va_ref, qb_ref, cmp_ref, vdup_ref, gb_ref,
                 kat_ref, nsat_ref, wint_ref, kdupt_ref):
    tm = x_ref.shape[0]
    hb = _rms(x_ref[...], g_ref[...]).astype(BF16)

    def cols(a, b):
        return _dot(hb, wr_ref[:, a:b])

    z = cols(0, 1024)
    for h in range(H_A):
        qa_ref[h] = z[:, h * LANES:(h + 1) * LANES].astype(BF16)
    z = cols(1024, 2048)
    for c in range(2 * KV_A):
        kva_ref[pl.ds(c, tm, stride=2 * KV_A), :] = z[:, c * LANES:(c + 1) * LANES]
    va_ref[...] = z[:, 512:1024].astype(BF16)
    z = cols(2048, 3072)
    for p in range(H_B // 2):
        qb_ref[p] = z[:, p * LANES:(p + 1) * LANES].astype(BF16)
    cmp_ref[...] = cols(3072, 3328)
    z = cols(3328, 3840)
    for k in range(4):
        vdup_ref[k] = z[:, k * LANES:(k + 1) * LANES].astype(BF16)
    gb_ref[...] = jax.nn.sigmoid(cols(3840, N_ROW))

    def rows(a, b):
        return _dot_nt(wt_ref[a:b, :], hb)

    z = rows(0, 512)
    for g in range(KV_A):
        kat_ref[g] = z[g * LANES:(g + 1) * LANES].astype(BF16)
    nsat_ref[...] = rows(512, 1024)
    wint_ref[...] = rows(1024, 1280)
    z = rows(1280, N_T)
    for k in range(4):
        kdupt_ref[k] = z[k * LANES:(k + 1) * LANES].astype(BF16)


def _proj(x, gain, w_row, w_t, tm):
    Bp, T, D = x.shape
    nt = T // tm
    n = Bp * T
    row = lambda b, i: (b * nt + i, 0)
    out_shape = (
        jax.ShapeDtypeStruct((Bp, H_A, T, LANES), BF16),
        jax.ShapeDtypeStruct((n * 2 * KV_A, LANES), F32),
        jax.ShapeDtypeStruct((n, KV_A * 2 * HD), BF16),
        jax.ShapeDtypeStruct((Bp, H_B // 2, T, LANES), BF16),
        jax.ShapeDtypeStruct((n, 4 * HD), F32),
        jax.ShapeDtypeStruct((Bp, 4, T, LANES), BF16),
        jax.ShapeDtypeStruct((n, LANES), F32),
        jax.ShapeDtypeStruct((Bp, KV_A, LANES, T), BF16),
        jax.ShapeDtypeStruct((Bp, 512, T), F32),
        jax.ShapeDtypeStruct((Bp, 256, T), F32),
        jax.ShapeDtypeStruct((Bp, 4, LANES, T), BF16),
    )
    out_specs = (
        pl.BlockSpec((None, H_A, tm, LANES), lambda b, i: (b, 0, i, 0)),
        pl.BlockSpec((tm * 2 * KV_A, LANES), row),
        pl.BlockSpec((tm, 512), row),
        pl.BlockSpec((None, H_B // 2, tm, LANES), lambda b, i: (b, 0, i, 0)),
        pl.BlockSpec((tm, 256), row),
        pl.BlockSpec((None, 4, tm, LANES), lambda b, i: (b, 0, i, 0)),
        pl.BlockSpec((tm, LANES), row),
        pl.BlockSpec((None, KV_A, LANES, tm), lambda b, i: (b, 0, 0, i)),
        pl.BlockSpec((None, 512, tm), lambda b, i: (b, 0, i)),
        pl.BlockSpec((None, 256, tm), lambda b, i: (b, 0, i)),
        pl.BlockSpec((None, 4, LANES, tm), lambda b, i: (b, 0, 0, i)),
    )
    return pl.pallas_call(
        _proj_kernel,
        grid=(Bp, nt),
        in_specs=[
            pl.BlockSpec((None, tm, D), lambda b, i: (b, i, 0)),
            pl.BlockSpec((1, D), lambda b, i: (0, 0)),
            _resident((D, N_ROW), lambda b, i: (0, 0)),
            _resident((N_T, D), lambda b, i: (0, 0)),
        ],
        out_specs=out_specs,
        out_shape=out_shape,
        compiler_params=_cparams(("parallel", "parallel")),
        name="proj",
    )(x, gain.reshape(1, D), w_row, w_t)


def _prep_in_weights(w_in_l):
    wt = w_in_l.T
    D = wt.shape[1]
    scale = HD ** -0.5
    qa, ka, va = wt[0:1024] * scale, wt[1024:1536], wt[1536:2048]
    qb = wt[2048:3072] * scale
    kvb = wt[3072:3840].reshape(6, KV_B, HD, D)
    gb = wt[3840:3888]
    gm = wt[3888:]
    dup = lambda w: jnp.concatenate([w, w], axis=0)
    vdup = jnp.concatenate([dup(kvb[k, g]) for k in (3, 5) for g in range(KV_B)], axis=0)
    kdup = jnp.concatenate([dup(kvb[k, g]) for k in (2, 4) for g in range(KV_B)], axis=0)
    w_row_t = jnp.concatenate(
        [qa, ka, va, qb, kvb[0:2].reshape(256, D), vdup, gb, jnp.zeros((LANES - gb.shape[0], D), F32)], axis=0)
    w_t = jnp.concatenate([ka, kvb[0:4].reshape(512, D), kvb[4:6].reshape(256, D), kdup], axis=0)
    return w_row_t.T.astype(BF16), w_t.astype(BF16), gm.T.astype(BF16)


def _softmax_step(s, v, m_ref, l_ref, acc_ref, idx):
    m_old = m_ref[idx]
    m_new = jnp.maximum(m_old, jnp.max(s, axis=-1, keepdims=True))
    p = jnp.exp(s - pltpu.repeat(m_new, s.shape[1] // LANES, axis=1))
    alpha = jnp.exp(m_old - m_new)
    l_ref[idx] = alpha * l_ref[idx] + jnp.sum(p, axis=-1, keepdims=True)
    acc_ref[idx] = alpha * acc_ref[idx] + _dot(p.astype(BF16), v)
    m_ref[idx] = m_new


def _toeplitz(u, n):
    H = u.shape[0]
    wp = jnp.pad(u[:, ::-1], ((0, 0), (0, 1)))
    r = jnp.broadcast_to(wp[:, None, :], (H, n, 2 * n)).reshape(H, 2 * n * n)[:, :n * (2 * n - 1)]
    return r.reshape(H, n, 2 * n - 1)[:, :, n - 1:]


def _near_tiles(table, t, masked):
    far = table[N_BUCKETS - 1][:, None]
    k = np.arange(2 * t - 1)
    d0 = jnp.where(jnp.asarray(k >= t - 1), _bias_of_dist(table, k - (t - 1)) - far, NEG if masked else 0.0)
    d1 = _bias_of_dist(table, k + 1) - far
    return jnp.stack([_toeplitz(d0, t), _toeplitz(d1, t)], axis=1)


def _diff_lambda(lp, lam_init):
    a = jnp.sum(lp[0:1] * lp[1:2], axis=-1, keepdims=True)
    b = jnp.sum(lp[2:3] * lp[3:4], axis=-1, keepdims=True)
    return jnp.exp(a) - jnp.exp(b) + lam_init


def _diff_finish(o0, o1, lp, subln, lam_init):
    o = o0 - _diff_lambda(lp, lam_init) * o1
    return _rms(o, subln) * (1.0 - lam_init)


def _dattn_kernel(q_ref, kt_ref, v_ref, bias_ref, lam_ref, sub_ref, o_ref, m_ref, l_ref, acc_ref, *, tq, lam_init):
    qi = pl.program_id(2)
    lane = lax.broadcasted_iota(jnp.int32, (tq, LANES), 1)
    m_ref[...] = jnp.full(m_ref.shape, NEG, F32)
    l_ref[...] = jnp.zeros(l_ref.shape, F32)
    acc_ref[...] = jnp.zeros(acc_ref.shape, F32)

    def step(start, width, kind):
        kt = kt_ref[:, pl.ds(start, width)]
        v = v_ref[pl.ds(start, width), :]
        for r in range(2):
            q = q_ref[r]
            zero = jnp.zeros_like(q)
            for c in range(2):
                s = _dot(jnp.where(lane < HD if c == 0 else lane >= HD, q, zero), kt)
                if kind is not None:
                    s = s + bias_ref[r, kind]
                _softmax_step(s, v, m_ref, l_ref, acc_ref, (c, slice(r * tq, (r + 1) * tq)))

    n_far = jnp.maximum(qi - 1, 0)

    def far_body(j, carry):
        step(pl.multiple_of(j * 2 * tq, 2 * tq), 2 * tq, None)
        return carry

    lax.fori_loop(0, n_far // 2, far_body, 0)

    @pl.when(n_far % 2 == 1)
    def _():
        step(pl.multiple_of((n_far - 1) * tq, tq), tq, None)

    @pl.when(qi >= 1)
    def _():
        step(pl.multiple_of((qi - 1) * tq, tq), tq, 1)

    step(pl.multiple_of(qi * tq, tq), tq, 0)

    o0 = acc_ref[0] / jnp.maximum(l_ref[0], 1e-30)
    o1 = acc_ref[1] / jnp.maximum(l_ref[1], 1e-30)
    y = _diff_finish(o0, o1, lam_ref[...], sub_ref[...], lam_init).astype(BF16)
    o_ref[:, 0:LANES] = y[:tq]
    o_ref[:, LANES:2 * LANES] = y[tq:]


def _dattn(qa4, kat, va, bias_a, lam_p, subln, lam_init, tq):
    B, _, T, _ = qa4.shape
    nq = T // tq
    tiles = _near_tiles(bias_a, tq, masked=True)
    return pl.pallas_call(
        functools.partial(_dattn_kernel, tq=tq, lam_init=lam_init),
        grid=(B, KV_A, nq),
        in_specs=[
            pl.BlockSpec((None, 2, tq, LANES), lambda b, g, i: (b, g, i, 0)),
            pl.BlockSpec((None, None, LANES, T), lambda b, g, i: (b, g, 0, 0)),
            pl.BlockSpec((T, LANES), lambda b, g, i: (b, g)),
            pl.BlockSpec((2, 2, tq, tq), lambda b, g, i: (g, 0, 0, 0)),
            pl.BlockSpec((4, HD), lambda b, g, i: (0, 0)),
            pl.BlockSpec((1, 2 * HD), lambda b, g, i: (0, 0)),
        ],
        out_specs=pl.BlockSpec((tq, 2 * LANES), lambda b, g, i: (b * nq + i, g)),
        out_shape=jax.ShapeDtypeStruct((B * T, H_A * 2 * HD), BF16),
        scratch_shapes=[pltpu.VMEM((2, 2 * tq, LANES), F32)] * 3,
        compiler_params=_cparams(("parallel", "parallel", "arbitrary")),
        name="dattn",
    )(qa4, kat, va, tiles, lam_p, subln.reshape(1, 2 * HD))


CMP_K = CMP_LEN * KV_B * HD


def _compress_core(xs_ref, nblk, pe_ref, w1_ref, w2_ref, p_ref):
    for j in range(CMP_LEN):
        xj = xs_ref[pl.ds(j, nblk, stride=CMP_STRIDE), :]
        p_ref[:, j * LANES:(j + 1) * LANES] = (xj + pe_ref[:, j * LANES:(j + 1) * LANES]).astype(BF16)
    hid = jax.nn.gelu(_dot(p_ref[...], w1_ref[...]))
    return _dot(hid.astype(BF16), w2_ref[...])


def _compress_kernel(x_ref, pe_ref, w1_ref, w2_ref, o_ref, xs_ref, p_ref, *, T):
    xs_ref[0:T, :] = x_ref[...]
    xs_ref[T:T + CMP_STRIDE, :] = jnp.zeros((CMP_STRIDE, LANES), F32)
    res = _compress_core(xs_ref, T // CMP_STRIDE, pe_ref, w1_ref, w2_ref, p_ref)
    o_ref[0] = res[:, 0:LANES].astype(BF16)
    o_ref[1] = res[:, LANES:2 * LANES].astype(BF16)


def _prep_cmp_weights(pe, w1, w2):
    pe_flat = jnp.broadcast_to(pe[:, :, None, :], (2, CMP_LEN, KV_B, HD)).reshape(2, 1, CMP_K)
    w1r = w1.reshape(2, CMP_LEN, HD, CMP_HID)
    w1bd = jnp.zeros((2, CMP_LEN, KV_B, HD, KV_B, CMP_HID), F32)
    w2bd = jnp.zeros((2, KV_B, CMP_HID, KV_B, 2, HD), F32)
    for g in range(KV_B):
        w1bd = w1bd.at[:, :, g, :, g, :].set(w1r)
        w2bd = w2bd.at[:, g, :, g, :, :].set(jnp.broadcast_to(w2[:, :, None, :], (2, CMP_HID, 2, HD)))
    return (pe_flat, w1bd.reshape(2, CMP_K, KV_B * CMP_HID).astype(BF16),
            w2bd.reshape(2, KV_B * CMP_HID, KV_B * 2 * HD).astype(BF16))


def _compress(cmp_rows, B, T, pe_flat, w1bd, w2bd):
    nblk = T // CMP_STRIDE
    return pl.pallas_call(
        functools.partial(_compress_kernel, T=T),
        grid=(B, 2),
        in_specs=[
            pl.BlockSpec((T, LANES), lambda b, k: (b, k)),
            pl.BlockSpec((None, 1, CMP_K), lambda b, k: (k, 0, 0)),
            pl.BlockSpec((None, CMP_K, 2 * CMP_HID), lambda b, k: (k, 0, 0)),
            pl.BlockSpec((None, 2 * CMP_HID, 2 * LANES), lambda b, k: (k, 0, 0)),
        ],
        out_specs=pl.BlockSpec((None, None, KV_B, nblk, LANES), lambda b, k: (b, k, 0, 0, 0)),
        out_shape=jax.ShapeDtypeStruct((B, 2, KV_B, nblk, LANES), BF16),
        scratch_shapes=[pltpu.VMEM((T + CMP_STRIDE, LANES), F32), pltpu.VMEM((nblk, CMP_K), BF16)],
        compiler_params=_cparams(("parallel", "parallel")),
        name="compress",
    )(cmp_rows, pe_flat, w1bd, w2bd)


def _group_head(g, rb):
    return R_B * g + 2 * (rb % 4) + rb // 4


def _stack_query(q4):
    lane = lax.broadcasted_iota(jnp.int32, q4.shape, 1)
    zero = jnp.zeros_like(q4)
    return jnp.concatenate([jnp.where(lane < HD, q4, zero), jnp.where(lane >= HD, q4, zero)], axis=0)


def _top_blocks(score, n):
    lanef = lax.broadcasted_iota(jnp.int32, score.shape, 1).astype(F32)
    width = float(score.shape[1])

    def body(_, carry):
        sc, sel = carry
        mx = jnp.max(sc, axis=1, keepdims=True)
        idx = jnp.min(jnp.where(sc == mx, lanef, width), axis=1, keepdims=True)
        hit = lanef == idx
        return jnp.where(hit, REMOVED, sc), jnp.where(hit, 1.0, sel)

    return lax.fori_loop(0, n, body, (score, jnp.zeros_like(score)))[1]


def _nsa_kernel(q_ref, gb_ref, kcv_ref, kselt_ref, kwint_ref, vsel_ref, vwin_ref, d_ref, ac_ref,
                c2s_ref, ex_ref, o_ref, s_ref, m_ref, l_ref, acc_ref, *, tq, tk, T):
    qi = pl.program_id(1)
    p0 = qi * tq
    nc = kcv_ref.shape[2]
    rows = R_B * tq
    half = rows // 2
    qpos = p0 + lax.broadcasted_iota(jnp.int32, (rows, 1), 0) % tq
    qpos_h = qpos[:half]
    lane4 = lax.broadcasted_iota(jnp.int32, (half, LANES), 1)

    gates = gb_ref[...]
    g_hi = gates.astype(BF16)
    g_lo = (gates - g_hi.astype(F32)).astype(BF16)
    gexp = [_dot(g_hi, ex_ref[br]) + _dot(g_lo, ex_ref[br]) for br in range(3)]

    def add_tile(hf, kind, off):
        for k in range(4):
            s_ref[k * tq:(k + 1) * tq, pl.ds(off, tq)] += d_ref[_group_head(g, 4 * hf + k), kind]

    st = p0 // CMP_STRIDE - BAND_LEAD
    ci = lax.broadcasted_iota(jnp.int32, (nc, LANES), 0)
    wl = lax.broadcasted_iota(jnp.int32, (nc, LANES), 1)
    place = jnp.where((ci == st + wl % BAND_W) & (wl < 2 * BAND_W), 1.0, 0.0).astype(BF16)
    cend = lax.broadcasted_iota(jnp.int32, (1, nc), 1) * CMP_STRIDE + (CMP_LEN - 1)
    mask = cend <= qpos
    blk = lax.broadcasted_iota(jnp.int32, (tq, LANES), 1)
    cur = (p0 + lax.broadcasted_iota(jnp.int32, (tq, 1), 0)) // SEL_BLOCK
    forced = (blk == 0) | (blk == cur) | (blk == cur - 1)
    Qs, o_cs, scores = [], [], []
    for g in range(KV_B):
        Q = _stack_query(q_ref[4 * g:4 * g + 4].reshape(4 * tq, LANES))
        s = _dot_nt(jnp.concatenate([Q, ac_ref[g]], axis=1), jnp.concatenate([kcv_ref[0, g], place], axis=1))
        s = jnp.where(mask, s, NEG)
        e = jnp.where(mask, jnp.exp(s - jnp.max(s, axis=-1, keepdims=True)), 0.0)
        pc = (e / jnp.maximum(jnp.sum(e, axis=-1, keepdims=True), 1e-30)).astype(BF16)
        oc = _dot(pc, kcv_ref[1, g])
        imp = jnp.sum(_dot(pc, c2s_ref[...]).reshape(R_B, tq, LANES), axis=0)
        Qs.append(Q)
        o_cs.append(jnp.where(lane4 < HD, oc[:half], oc[half:]))
        scores.append(jnp.where(blk <= cur, jnp.where(forced, FORCE, imp), NEG))
    sel = _top_blocks(jnp.concatenate(scores, axis=0), SEL_TOPN)

    for g in range(KV_B):
        o_c = o_cs[g]
        selneg = jnp.where((sel[g * tq:(g + 1) * tq] > 0.5) & (blk <= cur), 0.0, NEG).astype(BF16)
        Q = jnp.concatenate([Qs[g], jnp.concatenate([selneg] * R_B, axis=0)], axis=1)

        m_ref[...] = jnp.full(m_ref.shape, NEG, F32)
        l_ref[...] = jnp.zeros(l_ref.shape, F32)
        acc_ref[...] = jnp.zeros(acc_ref.shape, F32)

        def sel_block(j, mode, width=tk):
            ks = pl.multiple_of(j * width, width)
            bi = lax.broadcasted_iota(jnp.int32, (LANES, width), 0)
            kb = lax.broadcasted_iota(jnp.int32, (LANES, width), 1) // SEL_BLOCK
            expand = jnp.where(bi - kb == j * (width // SEL_BLOCK), 1.0, 0.0).astype(BF16)
            kt = jnp.concatenate([kselt_ref[g, :, pl.ds(ks, width)], expand], axis=0)
            v = vsel_ref[g, pl.ds(ks, width), :]
            for hf in range(2):
                s = _dot(Q[hf * half:(hf + 1) * half], kt)
                if mode:
                    s_ref[:, 0:tk] = s
                    if mode == 1:
                        add_tile(hf, 1, tk - tq)
                    else:
                        off0 = p0 - ks
                        add_tile(hf, 0, pl.multiple_of(off0, tq))

                        @pl.when(off0 >= tq)
                        def _():
                            add_tile(hf, 1, pl.multiple_of(off0 - tq, tq))

                    s = s_ref[:, 0:tk]
                    if mode == 2:
                        key = ks + lax.broadcasted_iota(jnp.int32, (1, tk), 1)
                        s = jnp.where(key <= qpos_h, s, NEG)
                _softmax_step(s, v, m_ref, l_ref, acc_ref, (0, slice(hf * half, (hf + 1) * half)))

        jd = p0 // tk
        at_block_start = (p0 % tk == 0) & (jd >= 1)

        n_far = jd - at_block_start.astype(jnp.int32)

        def far_body(j, carry):
            sel_block(j, 0, 2 * tk)
            return carry

        lax.fori_loop(0, n_far // 2, far_body, 0)

        @pl.when(n_far % 2 == 1)
        def _():
            sel_block(n_far - 1, 0)

        @pl.when(at_block_start)
        def _():
            sel_block(jd - 1, 1)

        sel_block(jd, 2)
        o = acc_ref[0] / jnp.maximum(l_ref[0], 1e-30)
        o_s = jnp.where(lane4 < HD, o[:half], o[half:])

        ws = pl.multiple_of(jnp.maximum(p0 - WINDOW, 0), tq)
        ww = WINDOW + tq
        kt = kwint_ref[g, :, pl.ds(ws, ww)]
        v = vwin_ref[g, pl.ds(ws, ww), :]
        off0 = p0 - ws
        dist = qpos_h - (ws + lax.broadcasted_iota(jnp.int32, (1, ww), 1))
        mask = (dist >= 0) & (dist <= WINDOW)
        o_half = []
        for hf in range(2):
            s_ref[...] = _dot(Qs[g][hf * half:(hf + 1) * half], kt)
            add_tile(hf, 0, pl.multiple_of(off0, tq))

            @pl.when(off0 >= tq)
            def _():
                add_tile(hf, 1, pl.multiple_of(off0 - tq, tq))

            s = jnp.where(mask, s_ref[...], NEG)
            e = jnp.where(mask, jnp.exp(s - jnp.max(s, axis=-1, keepdims=True)), 0.0)
            o_half.append(_dot(e.astype(BF16), v) / jnp.maximum(jnp.sum(e, axis=-1, keepdims=True), 1e-30))
        o_w = jnp.where(lane4 < HD, o_half[0], o_half[1])

        for p in range(4):
            c0 = (4 * g + p) * LANES
            r = slice(p * tq, (p + 1) * tq)
            o_ref[:, c0:c0 + LANES] = (gexp[0][:, c0:c0 + LANES] * o_c[r] + gexp[1][:, c0:c0 + LANES] * o_s[r]
                                       + gexp[2][:, c0:c0 + LANES] * o_w[r]).astype(BF16)


def _nsa_tables(bias_b, tq, nc, ns_pad):
    far = bias_b[N_BUCKETS - 1]
    i = np.arange(tq)[:, None]
    tiles = _near_tiles(bias_b, tq, masked=False)
    w = np.arange(BAND_W)[None, :]
    dist = i + CMP_STRIDE * BAND_LEAD - CMP_STRIDE * w - (CMP_LEN - 1)
    assert dist[:, 0].min() >= FAR_DIST and dist[:, -1].max() < 0
    band = jnp.where(jnp.asarray((dist >= 0) & (dist < FAR_DIST)),
                     _bias_of_dist(bias_b, dist) - far[:, None, None], 0.0)
    order = [[_group_head(g, rb) for rb in range(R_B)] for g in range(KV_B)]
    band = jnp.stack([jnp.concatenate([band[h] for h in order[g]], axis=0) for g in range(KV_B)])
    b_hi = band.astype(BF16)
    b_lo = (band - b_hi.astype(F32)).astype(BF16)
    ac = jnp.concatenate([b_hi, b_lo, jnp.zeros(band.shape[:2] + (LANES - 2 * BAND_W,), BF16)], axis=2)
    ci = np.arange(nc)[:, None] * CMP_STRIDE
    lo = np.arange(ns_pad)[None, :] * SEL_BLOCK
    c2s = jnp.asarray(((ci < lo + SEL_BLOCK) & (ci + CMP_LEN > lo)).astype(np.float32)).astype(BF16)
    k = np.arange(LANES)[:, None]
    col = np.arange(H_B * HD)[None, :]
    ex = jnp.asarray(np.stack([(k == (col // HD) * 3 + br) for br in range(3)]).astype(np.float32)).astype(BF16)
    return tiles, ac, c2s, ex


def _nsa(qb4, gb, kcv, kdupt, vdup, bias_b, tq, tk):
    B, _, T, _ = qb4.shape
    nq = T // tq
    nc = kcv.shape[3]
    assert T // SEL_BLOCK <= LANES and T >= WINDOW + tq and T % tk == 0
    tiles, ac, c2s, ex = _nsa_tables(bias_b, tq, nc, LANES)
    const = lambda *shape: pl.BlockSpec(shape, lambda b, i: (0,) * len(shape))
    return pl.pallas_call(
        functools.partial(_nsa_kernel, tq=tq, tk=tk, T=T),
        grid=(B, nq),
        in_specs=[
            pl.BlockSpec((None, H_B // 2, tq, LANES), lambda b, i: (b, 0, i, 0)),
            pl.BlockSpec((tq, LANES), lambda b, i: (b * nq + i, 0)),
            _resident((None, 2, KV_B, nc, LANES), lambda b, i: (b, 0, 0, 0, 0)),
            _resident((None, KV_B, LANES, T), lambda b, i: (b, 0, 0, 0)),
            _resident((None, KV_B, LANES, T), lambda b, i: (b, 1, 0, 0)),
            _resident((None, KV_B, T, LANES), lambda b, i: (b, 0, 0, 0)),
            _resident((None, KV_B, T, LANES), lambda b, i: (b, 1, 0, 0)),
            _resident((H_B, 2, tq, tq), lambda b, i: (0, 0, 0, 0)),
            _resident((KV_B, R_B * tq, LANES), lambda b, i: (0, 0, 0)),
            _resident((nc, LANES), lambda b, i: (0, 0)),
            _resident((3, LANES, H_B * HD), lambda b, i: (0, 0, 0)),
        ],
        out_specs=pl.BlockSpec((tq, H_B * HD), lambda b, i: (b * nq + i, 0)),
        out_shape=jax.ShapeDtypeStruct((B * T, H_B * HD), BF16),
        scratch_shapes=[pltpu.VMEM((R_B * tq // 2, WINDOW + tq), F32)] + [pltpu.VMEM((1, R_B * tq, LANES), F32)] * 3,
        compiler_params=_cparams(("parallel", "arbitrary")),
        name="nsa",
    )(qb4, gb, kcv, kdupt, kdupt, vdup, vdup, tiles, ac, c2s, ex)


def _merge_kernel(x_ref, g_ref, a_ref, b_ref, wgm_ref, wa_ref, wb_ref, wo_ref, o_ref):
    D = x_ref.shape[1]
    x = x_ref[...]
    hb = _rms(x, g_ref[...]).astype(BF16)
    ya = jax.nn.sigmoid(_dot(hb, wgm_ref[:, 0:D])) * _dot(a_ref[...], wa_ref[...])
    yb = jax.nn.sigmoid(_dot(hb, wgm_ref[:, D:2 * D])) * _dot(b_ref[...], wb_ref[...])
    o_ref[...] = x + _dot((ya + yb).astype(BF16), wo_ref[...])


def _merge(x2, gain, a, ob, w_gm, w_a, w_b, w_o, tm):
    n, D = x2.shape
    row = lambda i: (i, 0)
    return pl.pallas_call(
        _merge_kernel,
        grid=(n // tm,),
        in_specs=[
            pl.BlockSpec((tm, D), row),
            pl.BlockSpec((1, D), lambda i: (0, 0)),
            pl.BlockSpec((tm, a.shape[1]), row),
            pl.BlockSpec((tm, ob.shape[1]), row),
            _resident(w_gm.shape, lambda i: (0, 0)),
            _resident(w_a.shape, lambda i: (0, 0)),
            _resident(w_b.shape, lambda i: (0, 0)),
            _resident(w_o.shape, lambda i: (0, 0)),
        ],
        out_specs=pl.BlockSpec((tm, D), row),
        out_shape=jax.ShapeDtypeStruct((n, D), F32),
        compiler_params=_cparams(("parallel",)),
        name="merge",
    )(x2, gain.reshape(1, D), a, ob, w_gm, w_a, w_b, w_o)


def _swiglu_chunk(hb, w1_ref, w3_ref, w2_ref):
    t = jax.nn.silu(_dot(hb, w1_ref[...])) * _dot(hb, w3_ref[...])
    return _dot(t.astype(BF16), w2_ref[...])


def _ffn_kernel(x_ref, g_ref, w1_ref, w3_ref, w2_ref, gf_ref, o_ref, hb_ref, acc_ref, *, final_norm):
    f = pl.program_id(1)

    @pl.when(f == 0)
    def _():
        hb_ref[...] = _rms(x_ref[...], g_ref[...]).astype(BF16)
        acc_ref[...] = jnp.zeros(acc_ref.shape, F32)

    acc_ref[...] += _swiglu_chunk(hb_ref[...], w1_ref, w3_ref, w2_ref)

    @pl.when(f == pl.num_programs(1) - 1)
    def _():
        y = x_ref[...] + acc_ref[...]
        o_ref[...] = _rms(y, gf_ref[...]) if final_norm else y


def _ffn(x2, gain, w1, w3, w2, gain_final, final_norm, tm, tf):
    n, D = x2.shape
    F = w1.shape[1]
    return pl.pallas_call(
        functools.partial(_ffn_kernel, final_norm=final_norm),
        grid=(n // tm, F // tf),
        in_specs=[
            pl.BlockSpec((tm, D), lambda i, f: (i, 0)),
            pl.BlockSpec((1, D), lambda i, f: (0, 0)),
            pl.BlockSpec((D, tf), lambda i, f: (0, f)),
            pl.BlockSpec((D, tf), lambda i, f: (0, f)),
            pl.BlockSpec((tf, D), lambda i, f: (f, 0)),
            pl.BlockSpec((1, D), lambda i, f: (0, 0)),
        ],
        out_specs=pl.BlockSpec((tm, D), lambda i, f: (i, 0)),
        out_shape=jax.ShapeDtypeStruct((n, D), F32),
        scratch_shapes=[pltpu.VMEM((tm, D), BF16), pltpu.VMEM((tm, D), F32)],
        compiler_params=_cparams(("parallel", "arbitrary")),
        name="ffn",
    )(x2, gain.reshape(1, D), w1, w3, w2, gain_final.reshape(1, D))


def _moe_kernel(x_ref, g_ref, wr_ref, w1_ref, w3_ref, w2_ref, gf_ref, o_ref, hb_ref, gate_ref, acc_ref, *,
                final_norm):
    e = pl.program_id(1)
    f = pl.program_id(2)

    @pl.when((e == 0) & (f == 0))
    def _():
        hb = _rms(x_ref[...], g_ref[...]).astype(BF16)
        hb_ref[...] = hb
        acc_ref[...] = jnp.zeros(acc_ref.shape, F32)
        gate_ref[...] = _route_top2(hb, wr_ref)[0]

    lane = lax.broadcasted_iota(jnp.int32, gate_ref.shape, 1)
    gate_e = jnp.sum(jnp.where(lane == e, gate_ref[...], 0.0), axis=-1, keepdims=True)
    acc_ref[...] += gate_e * _swiglu_chunk(hb_ref[...], w1_ref, w3_ref, w2_ref)

    @pl.when((e == pl.num_programs(1) - 1) & (f == pl.num_programs(2) - 1))
    def _():
        y = x_ref[...] + acc_ref[...]
        o_ref[...] = _rms(y, gf_ref[...]) if final_norm else y


def _moe(x2, gain, w_r, w1, w3, w2, gain_final, final_norm, tm, tf):
    n, D = x2.shape
    E, _, F = w1.shape
    return pl.pallas_call(
        functools.partial(_moe_kernel, final_norm=final_norm),
        grid=(n // tm, E, F // tf),
        in_specs=[
            pl.BlockSpec((tm, D), lambda i, e, f: (i, 0)),
            pl.BlockSpec((1, D), lambda i, e, f: (0, 0)),
            pl.BlockSpec((D, LANES), lambda i, e, f: (0, 0)),
            pl.BlockSpec((None, D, tf), lambda i, e, f: (e, 0, f)),
            pl.BlockSpec((None, D, tf), lambda i, e, f: (e, 0, f)),
            pl.BlockSpec((None, tf, D), lambda i, e, f: (e, f, 0)),
            pl.BlockSpec((1, D), lambda i, e, f: (0, 0)),
        ],
        out_specs=pl.BlockSpec((tm, D), lambda i, e, f: (i, 0)),
        out_shape=jax.ShapeDtypeStruct((n, D), F32),
        scratch_shapes=[pltpu.VMEM((tm, D), BF16), pltpu.VMEM((tm, LANES), F32), pltpu.VMEM((tm, D), F32)],
        compiler_params=_cparams(("parallel", "arbitrary", "arbitrary")),
        name="moe",
    )(x2, gain.reshape(1, D), w_r, w1, w3, w2, gain_final.reshape(1, D))


MOE_SLOTS = 384


def _route_top2(hb, wr_ref):
    logits = _dot(hb, wr_ref[...])
    lane = lax.broadcasted_iota(jnp.int32, logits.shape, 1)
    lanef = lane.astype(F32)
    logits = jnp.where(lane < N_EXPERTS, logits, NEG)
    v1 = jnp.max(logits, axis=-1, keepdims=True)
    i1 = jnp.min(jnp.where(logits == v1, lanef, float(LANES)), axis=-1, keepdims=True)
    rest = jnp.where(lanef == i1, REMOVED, logits)
    v2 = jnp.max(rest, axis=-1, keepdims=True)
    i2 = jnp.min(jnp.where(rest == v2, lanef, float(LANES)), axis=-1, keepdims=True)
    e2 = jnp.exp(v2 - v1)
    gate = jnp.where(lanef == i1, 1.0 / (1.0 + e2), 0.0) + jnp.where(lanef == i2, e2 / (1.0 + e2), 0.0)
    return gate, jnp.where((lanef == i1) | (lanef == i2), 1.0, 0.0)


def _moe_routed_kernel(x_ref, g_ref, wr_ref, lt_ref, w1_ref, w3_ref, w2_ref, gf_ref, o_ref, hb_ref, gate_ref,
                       sel_ref, rank_ref, rankt_ref, selt_ref, xg_ref, ye_ref, cnt_ref, *, final_norm):
    e = pl.program_id(1)
    f = pl.program_id(2)
    last_f = pl.num_programs(2) - 1
    tm = x_ref.shape[0]
    n_pass, slots = xg_ref.shape[0], xg_ref.shape[1]
    lane = lax.broadcasted_iota(jnp.int32, (tm, LANES), 1)

    @pl.when((e == 0) & (f == 0))
    def _():
        x = x_ref[...]
        hb = _rms(x, g_ref[...]).astype(BF16)
        hb_ref[...] = hb
        o_ref[...] = x
        gate, chosen = _route_top2(hb, wr_ref)
        gate_ref[...] = gate
        sel_ref[...] = chosen
        rank = _dot(lt_ref[...], chosen.astype(BF16))
        rank_ref[...] = rank
        rankt_ref[...] = rank.T
        selt_ref[...] = chosen.T
        total = jnp.sum(chosen, axis=0, keepdims=True)
        for k in range(N_EXPERTS):
            cnt_ref[k] = total[0, k].astype(jnp.int32)

    def column(ref):
        return jnp.sum(jnp.where(lane == e, ref[...], 0.0), axis=-1, keepdims=True)

    for c in range(n_pass):
        @pl.when(c * slots < cnt_ref[e])
        def _():
            @pl.when(f == 0)
            def _():
                rk = rankt_ref[pl.ds(e, 1), :] - float(c * slots)
                srow = lax.broadcasted_iota(jnp.int32, (slots, tm), 0).astype(F32)
                take = jnp.where((rk == srow) & (selt_ref[pl.ds(e, 1), :] > 0.5), 1.0, 0.0).astype(BF16)
                xg_ref[c] = _dot(take, hb_ref[...]).astype(BF16)
                ye_ref[c] = jnp.zeros(ye_ref.shape[1:], F32)

            ye_ref[c] += _swiglu_chunk(xg_ref[c], w1_ref, w3_ref, w2_ref)

            @pl.when(f == last_f)
            def _():
                y = ye_ref[c]
                y_hi = y.astype(BF16)
                y_lo = (y - y_hi.astype(F32)).astype(BF16)
                scol = lax.broadcasted_iota(jnp.int32, (tm, slots), 1).astype(F32)
                put = jnp.where((column(rank_ref) - float(c * slots) == scol) & (column(sel_ref) > 0.5),
                                1.0, 0.0).astype(BF16)
                o_ref[...] += column(gate_ref) * (_dot(put, y_hi) + _dot(put, y_lo))

    if final_norm:
        @pl.when((e == pl.num_programs(1) - 1) & (f == last_f))
        def _():
            o_ref[...] = _rms(o_ref[...], gf_ref[...])


def _moe_routed(x2, gain, w_r, w1, w3, w2, gain_final, final_norm, tm, tf):
    n, D = x2.shape
    E, _, F = w1.shape
    slots = min(MOE_SLOTS, tm)
    n_pass = -(-tm // slots)
    lower = jnp.asarray(np.tril(np.ones((tm, tm), np.float32), -1)).astype(BF16)
    return pl.pallas_call(
        functools.partial(_moe_routed_kernel, final_norm=final_norm),
        grid=(n // tm, E, F // tf),
        in_specs=[
            _resident((tm, D), lambda i, e, f: (i, 0)),
            pl.BlockSpec((1, D), lambda i, e, f: (0, 0)),
            pl.BlockSpec((D, LANES), lambda i, e, f: (0, 0)),
            _resident((tm, tm), lambda i, e, f: (0, 0)),
            pl.BlockSpec((None, D, tf), lambda i, e, f: (e, 0, f)),
            pl.BlockSpec((None, D, tf), lambda i, e, f: (e, 0, f)),
            pl.BlockSpec((None, tf, D), lambda i, e, f: (e, f, 0)),
            pl.BlockSpec((1, D), lambda i, e, f: (0, 0)),
        ],
        out_specs=pl.BlockSpec((tm, D), lambda i, e, f: (i, 0)),
        out_shape=jax.ShapeDtypeStruct((n, D), F32),
        scratch_shapes=[pltpu.VMEM((tm, D), BF16), pltpu.VMEM((tm, LANES), F32), pltpu.VMEM((tm, LANES), F32),
                        pltpu.VMEM((tm, LANES), F32), pltpu.VMEM((LANES, tm), F32), pltpu.VMEM((LANES, tm), F32),
                        pltpu.VMEM((n_pass, slots, D), BF16), pltpu.VMEM((n_pass, slots, D), F32),
                        pltpu.SMEM((N_EXPERTS,), jnp.int32)],
        compiler_params=pltpu.CompilerParams(dimension_semantics=("parallel", "arbitrary", "arbitrary"),
                                             vmem_limit_bytes=MOE_VMEM_LIMIT),
        name="moe_routed",
    )(x2, gain.reshape(1, D), w_r, lower, w1, w3, w2, gain_final.reshape(1, D))


SROWS = 16


def _page_specs(block, kind_block, layer, n_fixed):
    tail = (0,) * (len(block) - 3)

    def spec(j):
        def index_map(b, c, pt, *_):
            return (layer, pt[b, c * PAGES_PER_STEP + j], kind_block) + tail
        return pl.BlockSpec(block, index_map)

    del n_fixed
    return [spec(j) for j in range(PAGES_PER_STEP)]


def _stack_rows(top, bot):
    pad = jnp.zeros((SROWS - top.shape[0] - bot.shape[0], LANES), F32)
    return jnp.concatenate([top, bot, pad], axis=0)


def _adec_kernel(pt_ref, q_ref, ks_ref, vs_ref, blast_ref, b0_ref, lam_ref, sub_ref, *rest, lam_init):
    pages = rest[:PAGES_PER_STEP]
    o_ref, qs_ref, m_ref, l_ref, acc_ref = rest[PAGES_PER_STEP:]
    c = pl.program_id(1)
    last = pl.num_programs(1) - 1
    lane2 = lax.broadcasted_iota(jnp.int32, (2, LANES), 1)

    @pl.when(c == 0)
    def _():
        for g in range(KV_A):
            qq = q_ref[2 * g:2 * g + 2]
            qg = _stack_rows(jnp.where(lane2 < HD, qq, 0.0), jnp.where(lane2 >= HD, qq, 0.0))
            qs_ref[g] = qg
            k_self = ks_ref[g:g + 1].astype(BF16).astype(F32)
            s_self = jnp.sum(qg.astype(BF16).astype(F32) * k_self, axis=-1, keepdims=True)
            m_ref[g] = s_self + b0_ref[g]
            l_ref[g] = jnp.ones((SROWS, LANES), F32)
            acc_ref[g] = jnp.broadcast_to(vs_ref[g:g + 1].astype(BF16).astype(F32), (SROWS, LANES))

    for g in range(KV_A):
        kg = jnp.concatenate([p[pl.ds(g, PAGE, stride=2 * KV_A), :] for p in pages], axis=0).astype(BF16)
        vg = jnp.concatenate([p[pl.ds(KV_A + g, PAGE, stride=2 * KV_A), :] for p in pages], axis=0).astype(BF16)
        s = _dot_nt(qs_ref[g].astype(BF16), kg)
        s = s + jnp.where(c == last, blast_ref[g], 0.0)
        _softmax_step(s, vg, m_ref, l_ref, acc_ref, g)

    @pl.when(c == last)
    def _():
        for g in range(KV_A):
            o = acc_ref[g] / jnp.maximum(l_ref[g], 1e-30)
            o_ref[2 * g:2 * g + 2, :] = _diff_finish(o[0:2], o[2:4], lam_ref[...], sub_ref[...], lam_init)


def _head_rows_a(vals):
    out = []
    for g in range(KV_A):
        r = [vals[2 * g], vals[2 * g + 1], vals[2 * g], vals[2 * g + 1]]
        out.append(jnp.stack(r + [jnp.zeros_like(vals[0])] * (SROWS - 4)))
    return jnp.stack(out)


def _head_rows_b(vals):
    out = []
    for g in range(KV_B):
        r = [vals[_group_head(g, rb)] for rb in range(R_B)]
        out.append(jnp.stack(r + [jnp.zeros_like(vals[0])] * (SROWS - R_B)))
    return jnp.stack(out)


def _adec(page_table, cache_view, layer, q, k_self, v_self, bias_a, lam_p, subln, lam_init):
    DB, NP = page_table.shape
    far = bias_a[N_BUCKETS - 1][:, None]
    kw = PAGES_PER_STEP * PAGE
    near = _bias_of_dist(bias_a, PAGE - np.arange(PAGE)) - far
    blast = _head_rows_a(jnp.pad(near, ((0, 0), (kw - PAGE, 0))))
    b0 = _head_rows_a((bias_a[0][:, None] - far) * jnp.ones((1, LANES), F32))
    per_b = lambda *blk: pl.BlockSpec((None,) + blk, lambda b, c, pt: (b,) + (0,) * len(blk))
    const = lambda *blk: pl.BlockSpec(blk, lambda b, c, pt: (0,) * len(blk))
    return pl.pallas_call(
        functools.partial(_adec_kernel, lam_init=lam_init),
        grid_spec=pltpu.PrefetchScalarGridSpec(
            num_scalar_prefetch=1,
            grid=(DB, NP // PAGES_PER_STEP),
            in_specs=[per_b(H_A, LANES), per_b(KV_A, LANES), per_b(KV_A, LANES),
                      const(KV_A, SROWS, kw), const(KV_A, SROWS, LANES), const(4, HD), const(1, 2 * HD)]
            + _page_specs((None, None, PAGE * 2 * KV_A, LANES), 0, layer, 0),
            out_specs=per_b(H_A, LANES),
            scratch_shapes=[pltpu.VMEM((KV_A, SROWS, LANES), F32)] * 4,
        ),
        out_shape=jax.ShapeDtypeStruct((DB, H_A, LANES), F32),
        compiler_params=_cparams(("parallel", "arbitrary")),
        name="adec",
    )(page_table, q, k_self, v_self, blast, b0, lam_p, subln.reshape(1, 2 * HD),
      *([cache_view] * PAGES_PER_STEP))


def _stack_pairs(q4):
    lane = lax.broadcasted_iota(jnp.int32, q4.shape, 1)
    return _stack_rows(jnp.where(lane < HD, q4, 0.0), jnp.where(lane >= HD, q4, 0.0))


def _top_indices(score, n):
    lanef = lax.broadcasted_iota(jnp.int32, score.shape, 1).astype(F32)
    slot = lax.broadcasted_iota(jnp.int32, (score.shape[0], LANES), 1)
    width = float(score.shape[1])

    def body(it, carry):
        sc, out = carry
        mx = jnp.max(sc, axis=1, keepdims=True)
        idx = jnp.min(jnp.where(sc == mx, lanef, width), axis=1, keepdims=True)
        return jnp.where(lanef == idx, REMOVED, sc), jnp.where(slot == it, idx.astype(jnp.int32), out)

    return lax.fori_loop(0, n, body, (score, jnp.zeros((score.shape[0], LANES), jnp.int32)))[1]


def _ncmp_kernel(pt_ref, q_ref, new_ref, bc_ref, pe_ref, w1_ref, w2_ref, c2s_ref, *rest, past):
    pages = rest[:PAGES_PER_STEP]
    oc_ref, idx_ref, xk_ref, xv_ref, p_ref = rest[PAGES_PER_STEP:]
    c = pl.program_id(1)
    stage = (xk_ref, xv_ref)
    for j, page in enumerate(pages):
        r0 = pl.multiple_of((c * PAGES_PER_STEP + j) * PAGE, PAGE)
        for kind in range(2):
            stage[kind][pl.ds(r0, PAGE), :] = page[kind].reshape(KV_B * HD, PAGE).T

    @pl.when(c == pl.num_programs(1) - 1)
    def _():
        nblk = past // CMP_STRIDE
        trow = lax.broadcasted_iota(jnp.int32, (2 * CMP_STRIDE, LANES), 0)
        res = []
        for kind in range(2):
            stage[kind][past:past + 2 * CMP_STRIDE, :] = jnp.where(trow == 0, new_ref[kind], 0.0)
            res.append(_compress_core(stage[kind], nblk, pe_ref.at[kind], w1_ref.at[kind], w2_ref.at[kind],
                                      p_ref).astype(BF16))
        ns_pad = c2s_ref.shape[1]
        for g in range(KV_B):
            qg = _stack_pairs(q_ref[4 * g:4 * g + 4]).astype(BF16)
            s = _dot_nt(qg, res[0][:, g * LANES:(g + 1) * LANES]) + bc_ref[g]
            e = jnp.exp(s - jnp.max(s, axis=-1, keepdims=True))
            pc = (e / jnp.maximum(jnp.sum(e, axis=-1, keepdims=True), 1e-30)).astype(BF16)
            oc_ref[g] = _dot(pc, res[1][:, g * LANES:(g + 1) * LANES])
            imp = jnp.sum(_dot(pc, c2s_ref[...])[0:R_B], axis=0, keepdims=True)
            blk = lax.broadcasted_iota(jnp.int32, (R_B, ns_pad), 1)
            cur = past // SEL_BLOCK
            forced = (blk == 0) | (blk == cur) | (blk == cur - 1)
            score = jnp.where(blk <= cur, jnp.where(forced, FORCE, jnp.broadcast_to(imp, (R_B, ns_pad))), NEG)
            idx_ref[g] = _top_indices(score, SEL_TOPN)


def _ncmp(page_table, nsa_view, layer, qb, new_cmp, bias_b, pe_flat, w1bd, w2bd):
    DB, NP = page_table.shape
    past = NP * PAGE
    nblk = past // CMP_STRIDE
    ns = past // SEL_BLOCK + 1
    ns_pad = -(-ns // LANES) * LANES
    i = np.arange(nblk)
    dist = past - (i * CMP_STRIDE + CMP_LEN - 1)
    bc = _head_rows_b(jnp.where(jnp.asarray(dist >= 0), _bias_of_dist(bias_b, dist), NEG))
    ci = i[:, None] * CMP_STRIDE
    lo = np.arange(ns_pad)[None, :] * SEL_BLOCK
    c2s = jnp.asarray(((ci < lo + SEL_BLOCK) & (ci + CMP_LEN > lo) & (lo < ns * SEL_BLOCK)).astype(np.float32))
    per_b = lambda *blk: pl.BlockSpec((None,) + blk, lambda b, c, pt: (b,) + (0,) * len(blk))
    const = lambda *blk: pl.BlockSpec(blk, lambda b, c, pt: (0,) * len(blk), pipeline_mode=pl.Buffered(1))
    return pl.pallas_call(
        functools.partial(_ncmp_kernel, past=past),
        grid_spec=pltpu.PrefetchScalarGridSpec(
            num_scalar_prefetch=1,
            grid=(DB, NP // PAGES_PER_STEP),
            in_specs=[per_b(H_B // 2, LANES), per_b(2, 1, LANES), const(KV_B, SROWS, nblk),
                      const(2, 1, CMP_K), const(2, CMP_K, 2 * CMP_HID), const(2, 2 * CMP_HID, 2 * LANES),
                      const(nblk, ns_pad)]
            + _page_specs((None, None, 2, KV_B, HD, PAGE), 0, layer, 0),
            out_specs=[per_b(KV_B, SROWS, LANES), per_b(KV_B, R_B, LANES)],
            scratch_shapes=[pltpu.VMEM((past + 2 * CMP_STRIDE, LANES), F32),
                            pltpu.VMEM((past + 2 * CMP_STRIDE, LANES), F32),
                            pltpu.VMEM((nblk, CMP_K), BF16)],
        ),
        out_shape=[jax.ShapeDtypeStruct((DB, KV_B, SROWS, LANES), F32),
                   jax.ShapeDtypeStruct((DB, KV_B, R_B, LANES), jnp.int32)],
        compiler_params=_cparams(("parallel", "arbitrary")),
        name="ncmp",
    )(page_table, qb, new_cmp, bc, pe_flat, w1bd, w2bd, c2s.astype(BF16), *([nsa_view] * PAGES_PER_STEP))


def _dup_t(x):
    return jnp.concatenate([x, x], axis=0).astype(BF16)


def _nsel_kernel(pt_ref, idx_ref, q_ref, gb_ref, oc_ref, new_ref, newcol_ref, blast_ref, b0_ref,
                 bw_ref, win_ref, ex_ref, *rest, n_pages):
    sel_pages = rest[:KV_B * SEL_TOPN]
    o_ref, wout_ref = rest[KV_B * SEL_TOPN:]
    b = pl.program_id(0)
    lane = lax.broadcasted_iota(jnp.int32, (SROWS, LANES), 1)
    lane4 = lax.broadcasted_iota(jnp.int32, (4, LANES), 1)
    g8 = jnp.broadcast_to(gb_ref[...], (8, LANES))
    g_hi = g8.astype(BF16)
    g_lo = (g8 - g_hi.astype(F32)).astype(BF16)
    gexp = [_dot(g_hi, ex_ref[br]) + _dot(g_lo, ex_ref[br]) for br in range(3)]

    def unstack(o):
        return jnp.where(lane4 < HD, o[0:4], o[4:8])

    def rounded(x):
        return x.astype(BF16).astype(F32)

    for g in range(KV_B):
        qg = _stack_pairs(q_ref[4 * g:4 * g + 4])
        qb = qg.astype(BF16)
        qr = rounded(qg)

        def self_score(k):
            return jnp.sum(qr * rounded(new_ref[k:k + 1]), axis=-1, keepdims=True) + b0_ref[g][:, 0:1]

        s_self = self_score(g)
        scores = []
        for t in range(SEL_TOPN):
            blk = idx_ref[b, g * SEL_TOPN + t]
            page = sel_pages[g * SEL_TOPN + t]
            s = _dot(qb, _dup_t(page[0, g]))
            s = s + jnp.where(blk // 2 == n_pages - 1, blast_ref[g], 0.0)
            ok = (blk < 2 * n_pages) & ((lane // SEL_BLOCK) == blk % 2)
            scores.append(jnp.where(ok, s, NEG))
        m = s_self
        for s in scores:
            m = jnp.maximum(m, jnp.max(s, axis=-1, keepdims=True))
        e_self = jnp.exp(s_self - m)
        l = e_self
        acc = rounded(e_self) * rounded(new_ref[4 + g:5 + g])
        for t, s in enumerate(scores):
            e = jnp.exp(s - m)
            l = l + jnp.sum(e, axis=-1, keepdims=True)
            acc = acc + _dot_nt(e.astype(BF16), _dup_t(sel_pages[g * SEL_TOPN + t][1, g]))
        o_s = unstack(acc / jnp.maximum(l, 1e-30))

        s_self = self_score(2 + g)
        s = _dot(qb, _dup_t(win_ref[0, g])) + bw_ref[g]
        m = jnp.maximum(s_self, jnp.max(s, axis=-1, keepdims=True))
        e = jnp.exp(s - m)
        e_self = jnp.exp(s_self - m)
        l = e_self + jnp.sum(e, axis=-1, keepdims=True)
        acc = rounded(e_self) * rounded(new_ref[6 + g:7 + g]) + _dot_nt(e.astype(BF16), _dup_t(win_ref[1, g]))
        o_w = unstack(acc / jnp.maximum(l, 1e-30))
        o_c = unstack(oc_ref[g])

        for p in range(4):
            c0 = (4 * g + p) * LANES
            o_ref[:, c0:c0 + LANES] = (gexp[0][0:1, c0:c0 + LANES] * o_c[p:p + 1]
                                       + gexp[1][0:1, c0:c0 + LANES] * o_s[p:p + 1]
                                       + gexp[2][0:1, c0:c0 + LANES] * o_w[p:p + 1])

    n_win = win_ref.shape[3]
    wl = lax.broadcasted_iota(jnp.int32, (HD, n_win), 1)
    for kv in range(2):
        for g in range(KV_B):
            r0 = (kv * KV_B + g) * HD
            shifted = pltpu.roll(win_ref[kv, g], n_win - 1, axis=1)
            wout_ref[kv, g] = jnp.where(wl == n_win - 1, newcol_ref[r0:r0 + HD, :], shifted)


def _nsel(page_table, idx, nsa_view, win_view, layer, qb, gates, oc, new_rows, new_col, bias_b):
    DB, NP = page_table.shape
    n_win = win_view.shape[-1]
    far = bias_b[N_BUCKETS - 1][:, None]
    blast = _head_rows_b(_bias_of_dist(bias_b, PAGE - np.arange(PAGE)) - far)
    b0 = _head_rows_b((bias_b[0][:, None] - far) * jnp.ones((1, LANES), F32))
    bw = _head_rows_b(_bias_of_dist(bias_b, n_win - np.arange(n_win)) - far)
    k = np.arange(LANES)[:, None]
    col = np.arange(H_B * HD)[None, :]
    ex = jnp.asarray(np.stack([(k == (col // HD) * 3 + br) for br in range(3)]).astype(np.float32)).astype(BF16)
    per_b = lambda *blk: pl.BlockSpec((None,) + blk, lambda b, pt, ix: (b,) + (0,) * len(blk))
    const = lambda *blk: pl.BlockSpec(blk, lambda b, pt, ix: (0,) * len(blk))

    def sel_spec(g, t):
        def index_map(b, pt, ix):
            return (layer, pt[b, jnp.clip(ix[b, g * SEL_TOPN + t] // 2, 0, NP - 1)], 1, 0, 0, 0)
        return pl.BlockSpec((None, None, 2, KV_B, HD, PAGE), index_map)

    return pl.pallas_call(
        functools.partial(_nsel_kernel, n_pages=NP),
        grid_spec=pltpu.PrefetchScalarGridSpec(
            num_scalar_prefetch=2,
            grid=(DB,),
            in_specs=[per_b(H_B // 2, LANES), per_b(1, LANES), per_b(KV_B, SROWS, LANES), per_b(8, LANES),
                      per_b(4 * HD, 1), const(KV_B, SROWS, LANES), const(KV_B, SROWS, LANES),
                      const(KV_B, SROWS, n_win),
                      pl.BlockSpec((None, None, 2, KV_B, HD, n_win), lambda b, pt, ix: (layer, b, 0, 0, 0, 0)),
                      const(3, LANES, H_B * HD)]
            + [sel_spec(g, t) for g in range(KV_B) for t in range(SEL_TOPN)],
            out_specs=[per_b(1, H_B * HD), per_b(2, KV_B, HD, n_win)],
        ),
        out_shape=[jax.ShapeDtypeStruct((DB, 1, H_B * HD), F32),
                   jax.ShapeDtypeStruct((DB, 2, KV_B, HD, n_win), F32)],
        compiler_params=_cparams(("parallel",)),
        name="nsel",
    )(page_table, idx, qb, gates, oc, new_rows, new_col, blast, b0, bw, win_view, ex,
      *([nsa_view] * (KV_B * SEL_TOPN)))


def _row_tile(n, cap):
    return n if n <= cap else cap


def _ff_tile(F):
    return F // 2 if (F > 1408 and (F // 2) % LANES == 0) else F


def kernel(x_prompt, x_sample, cache_attn_kv, cache_nsa_kv, state_win_kv, page_table, rel_bias, norm_mix, w_in, w_a, w_b, w_o, diff_lambda, diff_subln, cmp_pe, cmp_w1, cmp_w2, norm_ffn, ffn_w1, ffn_w3, ffn_w2, moe_router, moe_w1, moe_w3, moe_w2, norm_final):
    B, T, D = x_prompt.shape
    DB, dec_t, _ = x_sample.shape
    assert dec_t == 1
    depth = w_in.shape[0]
    n_pool = cache_attn_kv.shape[1]
    n_win = state_win_kv.shape[2]
    bias_a, bias_b = rel_bias[:, :H_A], rel_bias[:, H_A:]
    attn_view = cache_attn_kv.reshape(depth, n_pool, PAGE * 2 * KV_A, LANES)
    nsa_view = jnp.transpose(cache_nsa_kv, (0, 1, 3, 4, 5, 2))
    win_view = jnp.transpose(state_win_kv, (0, 1, 3, 4, 5, 2))

    xp = x_prompt.reshape(B * T, D)
    xs = x_sample.reshape(DB, D)
    tm = _row_tile(T, 512)
    att_t = _row_tile(T, ATT_T)
    outs = [[] for _ in range(6)]
    for l in range(depth):
        lam_init = 0.8 - 0.6 * math.exp(-0.3 * l)
        last = l == depth - 1
        w_row, w_t, w_gm = _prep_in_weights(w_in[l])
        wa, wb, wo = w_a[l].astype(BF16), w_b[l].astype(BF16), w_o[l].astype(BF16)
        pe_flat, w1bd, w2bd = _prep_cmp_weights(cmp_pe[l], cmp_w1[l], cmp_w2[l])

        qa4, kva, va, qb4, cmp_rows, vdup, gb, kat, nsat, wint, kdupt = _proj(
            xp.reshape(B, T, D), norm_mix[l], w_row, w_t, tm)
        a = _dattn(qa4, kat, va, bias_a, diff_lambda[l], diff_subln[l], lam_init, att_t)
        kcv = _compress(cmp_rows, B, T, pe_flat, w1bd, w2bd)
        ob = _nsa(qb4, gb, kcv, kdupt, vdup, bias_b, NSA_TQ, NSA_TK)
        xp = _merge(xp, norm_mix[l], a, ob, w_gm, wa, wb, wo, tm)
        outs[0].append(kva.reshape(B, T, 2, KV_A, 2 * HD))
        outs[2].append(jnp.transpose(nsat.reshape(B, 4, KV_B, HD, T), (0, 4, 1, 2, 3)))
        wlen = min(WINDOW, T)
        outs[4].append(jnp.transpose(wint[:, :, T - wlen:].reshape(B, 2, KV_B, HD, wlen), (0, 4, 1, 2, 3)))

        qa_s, kva_s, _, qb_s, cmp_s, vdup_s, gb_s, _, nsat_s, wint_s, kdupt_s = _proj(
            xs.reshape(1, DB, D), norm_mix[l], w_row, w_t, DB)
        kva_s = kva_s.reshape(DB, 2, KV_A, 2 * HD)
        q_a = jnp.transpose(qa_s[0], (1, 0, 2)).astype(F32)
        q_b = jnp.transpose(qb_s[0], (1, 0, 2)).astype(F32)
        a_s = _adec(page_table, attn_view, l, q_a, kva_s[:, 0], kva_s[:, 1], bias_a, diff_lambda[l],
                    diff_subln[l], lam_init)
        oc, idx = _ncmp(page_table, nsa_view, l, q_b, cmp_s.reshape(DB, 2, 1, LANES), bias_b, pe_flat, w1bd, w2bd)
        new_rows = jnp.concatenate([jnp.transpose(kdupt_s[0], (2, 0, 1)), jnp.transpose(vdup_s[0], (1, 0, 2))],
                                   axis=1).astype(F32)
        new_col = jnp.transpose(wint_s[0]).reshape(DB, 4 * HD, 1)
        ob_s, wout = _nsel(page_table, idx[:, :, 0, :SEL_TOPN].reshape(DB, KV_B * SEL_TOPN), nsa_view, win_view, l, q_b,
                           gb_s.reshape(DB, 1, LANES), oc, new_rows, new_col, bias_b)
        xs = _merge(xs, norm_mix[l], a_s.reshape(DB, H_A * 2 * HD).astype(BF16),
                    ob_s.reshape(DB, H_B * HD).astype(BF16), w_gm, wa, wb, wo, DB)
        outs[1].append(kva_s.reshape(DB, 1, 2, KV_A, 2 * HD))
        outs[3].append(jnp.transpose(nsat_s[0]).reshape(DB, 1, 4, KV_B, HD))
        outs[5].append(jnp.transpose(wout, (0, 4, 1, 2, 3)))

        e = l // 2
        if l % 2 == 0:
            w1, w3, w2 = ffn_w1[e].astype(BF16), ffn_w3[e].astype(BF16), ffn_w2[e].astype(BF16)
            tf = _ff_tile(w1.shape[1])
            xp = _ffn(xp, norm_ffn[l], w1, w3, w2, norm_final, last, tm, tf)
            xs = _ffn(xs, norm_ffn[l], w1, w3, w2, norm_final, last, DB, tf)
        else:
            w1, w3, w2 = moe_w1[e].astype(BF16), moe_w3[e].astype(BF16), moe_w2[e].astype(BF16)
            w_r = jnp.pad(moe_router[e], ((0, 0), (0, LANES - N_EXPERTS))).astype(BF16)
            tf = _ff_tile(w1.shape[2])
            xp = _moe_routed(xp, norm_ffn[l], w_r, w1, w3, w2, norm_final, last, _row_tile(B * T, 1024), tf)
            xs = _moe(xs, norm_ffn[l], w_r, w1, w3, w2, norm_final, last, DB, tf)

    return (xp.reshape(B, T, D), xs.reshape(DB, 1, D)) + tuple(jnp.stack(o) for o in outs)
```
